```python
import math
import jax, jax.numpy as jnp
from jax import lax
import numpy as np

D_MODEL = 4096
BATCH = 2
SEQ = 4096
DEPTH = 2

N_A_LAYERS = DEPTH // 2
N_B_LAYERS = DEPTH - N_A_LAYERS
SSM_W = D_MODEL // 2
SSM_GROUP = 16
SSM_GROUPS = SSM_W // SSM_GROUP
SSM_STATE = 64
SB_HEAD_DIM = 128
SB_HEADS = SSM_W // SB_HEAD_DIM
SB_W = SB_HEADS * SB_HEAD_DIM
MEM_TOKENS = 256
MEM_HEADS = 4
MEM_HEAD_DIM = D_MODEL // 16
MEM_W = MEM_HEADS * MEM_HEAD_DIM
MIX_W = SSM_W + MEM_W
D_FF = 256 * (-(-8 * D_MODEL // (3 * 256)))
CONV_W = 3
Q_BLOCK = 128
EPS = 1e-6

kernel_name = "yoco_s5_stickbreaking_memory_hybrid"


def _rms(x, g):
    x32 = x.astype(jnp.float32)
    y = x32 * lax.rsqrt(jnp.mean(x32 * x32, axis=-1, keepdims=True) + EPS)
    return (y * g.astype(jnp.float32)).astype(x.dtype)


def s5_mixer(u, lam_re, lam_im, log_step, b_re, b_im, c_re, c_im, d_skip, w_glu):
    f32 = jnp.float32
    bsz, seq, _ = u.shape
    u32 = u.astype(f32).reshape(bsz, seq, SSM_GROUPS, SSM_GROUP)
    lr = jnp.minimum(lam_re.astype(f32), -1e-4)
    li = lam_im.astype(f32)
    step = jnp.exp(log_step.astype(f32))[:, None]
    mag = jnp.exp(lr * step)
    ab_re = mag * jnp.cos(li * step)
    ab_im = mag * jnp.sin(li * step)
    p_re = ab_re - 1.0
    den = lr * lr + li * li
    f_re = (p_re * lr + ab_im * li) / den
    f_im = (ab_im * lr - p_re * li) / den
    br = b_re.astype(f32)
    bi = b_im.astype(f32)
    bb_re = f_re[..., None] * br - f_im[..., None] * bi
    bb_im = f_re[..., None] * bi + f_im[..., None] * br
    bu_re = jnp.einsum("bsgh,gph->bsgp", u32, bb_re)
    bu_im = jnp.einsum("bsgh,gph->bsgp", u32, bb_im)
    a_re = jnp.broadcast_to(ab_re, bu_re.shape)
    a_im = jnp.broadcast_to(ab_im, bu_im.shape)

    def combine(left, right):
        a1r, a1i, b1r, b1i = left
        a2r, a2i, b2r, b2i = right
        return (a2r * a1r - a2i * a1i,
                a2r * a1i + a2i * a1r,
                a2r * b1r - a2i * b1i + b2r,
                a2r * b1i + a2i * b1r + b2i)

    _, _, xr, xi = lax.associative_scan(combine, (a_re, a_im, bu_re, bu_im), axis=1)
    y = (jnp.einsum("bsgp,ghp->bsgh", xr, c_re.astype(f32))
         - jnp.einsum("bsgp,ghp->bsgh", xi, c_im.astype(f32))
         + d_skip.astype(f32) * u32)
    g = jax.nn.gelu(y.reshape(bsz, seq, SSM_W))
    out = g * jax.nn.sigmoid(g @ w_glu.astype(f32))
    return out.astype(u.dtype)


def stick_breaking(q, k, v):
    bsz, seq, nh, hd = q.shape
    scale = 1.0 / math.sqrt(hd)
    outs = []
    for i in range(seq // Q_BLOCK):
        t0 = i * Q_BLOCK
        t1 = t0 + Q_BLOCK
        qb = q[:, t0:t1]
        kb = k[:, :t1]
        vb = v[:, :t1]
        z = jnp.einsum("bqhd,bkhd->bhqk", qb, kb).astype(jnp.float32) * scale
        tpos = t0 + jnp.arange(Q_BLOCK)[:, None]
        spos = jnp.arange(t1)[None, :]
        causal = spos < tpos
        log_1mb = jnp.where(causal, jax.nn.log_sigmoid(-z), 0.0)
        log_rem = lax.cumsum(log_1mb, axis=3, reverse=True) - log_1mb
        w = jnp.where(causal, jnp.exp(jax.nn.log_sigmoid(z) + log_rem), 0.0)
        outs.append(jnp.einsum("bhqk,bkhd->bqhd", w.astype(v.dtype), vb))
    return jnp.concatenate(outs, axis=1).reshape(bsz, seq, nh * hd)


def mem_attention(qm, mem_n, w_kv, gq, gk):
    bsz, seq, _ = qm.shape
    q = _rms(qm.reshape(bsz, seq, MEM_HEADS, MEM_HEAD_DIM), gq)
    kv = mem_n @ w_kv
    k, v = jnp.split(kv, 2, axis=-1)
    k = _rms(k.reshape(bsz, -1, MEM_HEADS, MEM_HEAD_DIM), gk)
    v = v.reshape(bsz, -1, MEM_HEADS, MEM_HEAD_DIM)
    logits = jnp.einsum("bshd,bmhd->bhsm", q, k).astype(jnp.float32) / math.sqrt(MEM_HEAD_DIM)
    p = jax.nn.softmax(logits, axis=-1)
    o = jnp.einsum("bhsm,bmhd->bshd", p.astype(v.dtype), v)
    return o.reshape(bsz, seq, MEM_W)


def conv_ffn(h, w_up, conv_w, conv_b, w_down):
    up = h @ w_up
    up = lax.conv_general_dilated(
        up, conv_w[:, None, :], window_strides=(1,), padding=[(CONV_W - 1, 0)],
        dimension_numbers=("NWC", "WIO", "NWC"), feature_group_count=up.shape[-1]) + conv_b
    a, b = jnp.split(up, 2, axis=-1)
    return (jax.nn.silu(a) * b) @ w_down


def setup_inputs(seed: int = 0) -> dict:
    key = jax.random.key(seed)
    ks = jax.random.split(key, 26)
    f32 = jnp.float32

    def nrm(k, shape, scale):
        return jax.random.normal(k, shape, f32) * scale

    def gain(k, shape):
        return 1.0 + 0.02 * jax.random.normal(k, shape, f32)

    n_idx = jnp.arange(SSM_STATE, dtype=f32)
    sshape = (N_A_LAYERS, SSM_GROUPS, SSM_STATE)
    return {
        "x": nrm(ks[0], (BATCH, SEQ, D_MODEL), 1.0),
        "mem": nrm(ks[1], (BATCH, MEM_TOKENS, D_MODEL), 1.0),
        "norm_mix_g": gain(ks[2], (DEPTH, D_MODEL)),
        "w_in": nrm(ks[3], (DEPTH, D_MODEL, MIX_W), D_MODEL ** -0.5),
        "w_out": nrm(ks[4], (DEPTH, MIX_W, D_MODEL), MIX_W ** -0.5),
        "mem_norm_g": gain(ks[5], (DEPTH, D_MODEL)),
        "w_mem_kv": nrm(ks[6], (DEPTH, D_MODEL, 2 * MEM_W), D_MODEL ** -0.5),
        "mem_q_norm_g": gain(ks[7], (DEPTH, MEM_HEAD_DIM)),
        "mem_k_norm_g": gain(ks[8], (DEPTH, MEM_HEAD_DIM)),
        "s5_lam_re": -0.5 + 0.01 * jax.random.normal(ks[9], sshape, f32),
        "s5_lam_im": math.pi * n_idx + 0.01 * jax.random.normal(ks[10], sshape, f32),
        "s5_log_step": jax.random.uniform(ks[11], (N_A_LAYERS, SSM_GROUPS), f32,
                                          math.log(1e-3), math.log(1e-1)),
        "s5_b_re": nrm(ks[12], (N_A_LAYERS, SSM_GROUPS, SSM_STATE, SSM_GROUP), (2 * SSM_GROUP) ** -0.5),
        "s5_b_im": nrm(ks[13], (N_A_LAYERS, SSM_GROUPS, SSM_STATE, SSM_GROUP), (2 * SSM_GROUP) ** -0.5),
        "s5_c_re": nrm(ks[14], (N_A_LAYERS, SSM_GROUPS, SSM_GROUP, SSM_STATE), (2 * SSM_STATE) ** -0.5),
        "s5_c_im": nrm(ks[15], (N_A_LAYERS, SSM_GROUPS, SSM_GROUP, SSM_STATE), (2 * SSM_STATE) ** -0.5),
        "s5_d": nrm(ks[16], (N_A_LAYERS, SSM_GROUPS, SSM_GROUP), 1.0),
        "s5_w_glu": nrm(ks[17], (N_A_LAYERS, SSM_W, SSM_W), SSM_W ** -0.5),
        "kv_norm_g": gain(ks[18], (D_MODEL,)),
        "w_kv_shared": nrm(ks[19], (D_MODEL, 2 * SB_W), D_MODEL ** -0.5),
        "norm_ffn_g": gain(ks[20], (DEPTH, D_MODEL)),
        "w_ffn_up": nrm(ks[21], (DEPTH, D_MODEL, 2 * D_FF), D_MODEL ** -0.5),
        "ffn_conv_w": nrm(ks[22], (DEPTH, CONV_W, 2 * D_FF), CONV_W ** -0.5),
        "ffn_conv_b": nrm(ks[23], (DEPTH, 2 * D_FF), 0.02),
        "w_ffn_down": nrm(ks[24], (DEPTH, D_FF, D_MODEL), D_FF ** -0.5),
    }


def reference(x, mem, norm_mix_g, w_in, w_out, mem_norm_g, w_mem_kv, mem_q_norm_g,
              mem_k_norm_g, s5_lam_re, s5_lam_im, s5_log_step, s5_b_re, s5_b_im,
              s5_c_re, s5_c_im, s5_d, s5_w_glu, kv_norm_g, w_kv_shared, norm_ffn_g,
              w_ffn_up, ffn_conv_w, ffn_conv_b, w_ffn_down):
    bsz, seq, _ = x.shape
    h = x
    k_sh = None
    v_sh = None
    for i in range(DEPTH):
        proj = _rms(h, norm_mix_g[i]) @ w_in[i]
        prim = proj[..., :SSM_W]
        qm = proj[..., SSM_W:]
        m_out = mem_attention(qm, _rms(mem, mem_norm_g[i]), w_mem_kv[i],
                              mem_q_norm_g[i], mem_k_norm_g[i])
        if i < N_A_LAYERS:
            p_out = s5_mixer(prim, s5_lam_re[i], s5_lam_im[i], s5_log_step[i],
                             s5_b_re[i], s5_b_im[i], s5_c_re[i], s5_c_im[i],
                             s5_d[i], s5_w_glu[i])
        else:
            q = prim.reshape(bsz, seq, SB_HEADS, SB_HEAD_DIM)
            p_out = stick_breaking(q, k_sh, v_sh)
        h = h + jnp.concatenate([p_out, m_out], axis=-1) @ w_out[i]
        h = h + conv_ffn(_rms(h, norm_ffn_g[i]), w_ffn_up[i], ffn_conv_w[i],
                         ffn_conv_b[i], w_ffn_down[i])
        if i == N_A_LAYERS - 1:
            kv = _rms(h, kv_norm_g) @ w_kv_shared
            k_sh, v_sh = jnp.split(kv, 2, axis=-1)
            k_sh = k_sh.reshape(bsz, seq, SB_HEADS, SB_HEAD_DIM)
            v_sh = v_sh.reshape(bsz, seq, SB_HEADS, SB_HEAD_DIM)
    return h
```

```python
import functools
import math

import jax
import jax.numpy as jnp
from jax import lax
from jax.experimental import pallas as pl
from jax.experimental.pallas import tpu as pltpu

F32 = jnp.float32
BF16 = jnp.bfloat16
EPS = 1e-6

V7X_VMEM_BYTES = 64 * 1024 * 1024
VMEM_LIMIT_BYTES = V7X_VMEM_BYTES - 8 * 1024 * 1024
SUBLANES = 8
LANES = 128

SSM_GROUP = 16
SSM_STATE = 64
S5_CHUNK = 16
SB_HEAD_DIM = 128
MEM_HEADS = 4
CONV_W = 3


def _params(*semantics):
    return pltpu.CompilerParams(dimension_semantics=semantics,
                                vmem_limit_bytes=VMEM_LIMIT_BYTES)


def _rms_kernel(x_ref, g_ref, o_ref):
    x = x_ref[...]
    ms = jnp.mean(x * x, axis=-1, keepdims=True)
    o_ref[...] = (x * lax.rsqrt(ms + EPS) * g_ref[...]).astype(o_ref.dtype)


def rmsnorm(x, g, tm=256):
    m, d = x.shape
    tm = min(tm, m)
    return pl.pallas_call(
        _rms_kernel,
        grid=(m // tm,),
        in_specs=[pl.BlockSpec((tm, d), lambda i: (i, 0)),
                  pl.BlockSpec((1, d), lambda i: (0, 0))],
        out_specs=pl.BlockSpec((tm, d), lambda i: (i, 0)),
        out_shape=jax.ShapeDtypeStruct((m, d), BF16),
        compiler_params=_params("parallel"),
        name="rmsnorm",
    )(x, g.reshape(1, d))


def _mm_kernel(*refs, n_lhs, has_res, nk):
    a_refs = refs[:n_lhs]
    w_refs = refs[n_lhs:2 * n_lhs]
    r_ref = refs[2 * n_lhs] if has_res else None
    o_ref = refs[-1]
    acc = None
    for a_ref, w_ref in zip(a_refs, w_refs):
        d = jnp.dot(a_ref[...], w_ref[...].astype(BF16), preferred_element_type=F32)
        acc = d if acc is None else acc + d
    if nk == 1:
        if has_res:
            acc = r_ref[...] + acc
        o_ref[...] = acc.astype(o_ref.dtype)
    else:
        k = pl.program_id(2)

        @pl.when(k == 0)
        def _():
            o_ref[...] = (r_ref[...] + acc) if has_res else acc

        @pl.when(k > 0)
        def _():
            o_ref[...] += acc


def matmul(lhs, w, res=None, out_dtype=F32, tm=1024, tn=512, tk=None, name="matmul"):
    m = lhs[0].shape[0]
    n = w.shape[1]
    tm = min(tm, m)
    tn = min(tn, n)
    n_lhs = len(lhs)
    in_specs, args = [], []
    if tk is None:
        nk = 1
        for a in lhs:
            in_specs.append(pl.BlockSpec((tm, a.shape[1]), lambda j, i: (i, 0)))
            args.append(a)
        row = 0
        for a in lhs:
            ka = a.shape[1]
            assert row % ka == 0
            in_specs.append(pl.BlockSpec((ka, tn), lambda j, i, b=row // ka: (b, j)))
            args.append(w)
            row += ka
        assert row == w.shape[0]
        if res is not None:
            in_specs.append(pl.BlockSpec((tm, tn), lambda j, i: (i, j)))
            args.append(res)
        out_spec = pl.BlockSpec((tm, tn), lambda j, i: (i, j))
        grid = (n // tn, m // tm)
        sem = ("parallel", "parallel")
    else:
        assert n_lhs == 1 and out_dtype == F32
        kdim = w.shape[0]
        nk = kdim // tk
        in_specs.append(pl.BlockSpec((tm, tk), lambda j, i, k: (i, k)))
        args.append(lhs[0])
        in_specs.append(pl.BlockSpec((tk, tn), lambda j, i, k: (k, j)))
        args.append(w)
        if res is not None:
            in_specs.append(pl.BlockSpec((tm, tn), lambda j, i, k: (i, j)))
            args.append(res)
        out_spec = pl.BlockSpec((tm, tn), lambda j, i, k: (i, j))
        grid = (n // tn, m // tm, nk)
        sem = ("parallel", "parallel", "arbitrary")
    return pl.pallas_call(
        functools.partial(_mm_kernel, n_lhs=n_lhs, has_res=res is not None, nk=nk),
        grid=grid,
        in_specs=in_specs,
        out_specs=out_spec,
        out_shape=jax.ShapeDtypeStruct((m, n), out_dtype),
        compiler_params=_params(*sem),
        name=name,
    )(*args)


def _mem_attn_kernel(q_ref, k_ref, v_ref, gq_ref, gk_ref, o_ref, *, heads, hd):
    gq = gq_ref[...]
    gk = gk_ref[...]
    inv_sqrt = 1.0 / math.sqrt(hd)
    for h in range(heads):
        sl = slice(h * hd, (h + 1) * hd)
        q = q_ref[:, sl]
        k = k_ref[:, sl]
        qn = (q * lax.rsqrt(jnp.mean(q * q, axis=-1, keepdims=True) + EPS) * gq).astype(BF16)
        kn = (k * lax.rsqrt(jnp.mean(k * k, axis=-1, keepdims=True) + EPS) * gk).astype(BF16)
        logits = lax.dot_general(qn, kn, (((1,), (1,)), ((), ())),
                                 preferred_element_type=F32) * inv_sqrt
        mx = jnp.max(logits, axis=-1, keepdims=True)
        p = jnp.exp(logits - mx)
        p = p / jnp.sum(p, axis=-1, keepdims=True)
        o = jnp.dot(p.astype(BF16), v_ref[:, sl].astype(BF16), preferred_element_type=F32)
        o_ref[:, sl] = o.astype(o_ref.dtype)


def mem_attention(proj, kv_mem, gq, gk, *, bsz, q_col_block, ts=512):
    t = proj.shape[0]
    seq = t // bsz
    mem_w = kv_mem.shape[1] // 2
    mem_tokens = kv_mem.shape[0] // bsz
    hd = mem_w // MEM_HEADS
    ts = min(ts, seq)
    nt = seq // ts
    return pl.pallas_call(
        functools.partial(_mem_attn_kernel, heads=MEM_HEADS, hd=hd),
        grid=(bsz, nt),
        in_specs=[pl.BlockSpec((ts, mem_w), lambda b, i: (b * nt + i, q_col_block)),
                  pl.BlockSpec((mem_tokens, mem_w), lambda b, i: (b, 0)),
                  pl.BlockSpec((mem_tokens, mem_w), lambda b, i: (b, 1)),
                  pl.BlockSpec((1, hd), lambda b, i: (0, 0)),
                  pl.BlockSpec((1, hd), lambda b, i: (0, 0))],
        out_specs=pl.BlockSpec((ts, mem_w), lambda b, i: (b * nt + i, 0)),
        out_shape=jax.ShapeDtypeStruct((t, mem_w), BF16),
        compiler_params=_params("parallel", "parallel"),
        name="mem_attention",
    )(proj, kv_mem, kv_mem, gq.reshape(1, hd), gk.reshape(1, hd))


def _half_sign(shape, first, second):
    lane = lax.broadcasted_iota(jnp.int32, shape, len(shape) - 1)
    return jnp.where(lane < SSM_STATE, first, second).astype(F32)


def _s5_prep_kernel(lr_ref, li_ref, ls_ref, b2_ref, b2s_ref, c2_ref, c2s_ref,
                    win_ref, wo_ref, tt_ref, dec_ref, *, n_dec):
    L = S5_CHUNK
    H = SSM_GROUP
    lr = jnp.minimum(lr_ref[0], -1e-4)
    li = li_ref[0]
    step = jnp.exp(ls_ref[0])
    mag = jnp.exp(lr * step)
    ab_re = mag * jnp.cos(li * step)
    ab_im = mag * jnp.sin(li * step)
    p_re = ab_re - 1.0
    den = lr * lr + li * li
    f_re = (p_re * lr + ab_im * li) / den
    f_im = (ab_im * lr - p_re * li) / den
    bb2 = f_re * b2_ref[0] + f_im * (b2s_ref[0] * _half_sign((H, LANES), -1.0, 1.0))
    bb2s = jnp.concatenate([bb2[:, SSM_STATE:], bb2[:, :SSM_STATE]], axis=1)
    bb2s = bb2s * _half_sign((H, LANES), -1.0, 1.0)
    c_a = c2_ref[0] * _half_sign((H, LANES), 1.0, -1.0)
    c_b = -c2s_ref[0]

    def powers(tt):
        m = jnp.exp(lr * step * tt)
        return m * jnp.cos(li * step * tt), m * jnp.sin(li * step * tt)

    tcol = lax.broadcasted_iota(jnp.int32, (L, 1), 0).astype(F32)
    ar, ai = powers(float(L - 1) - tcol)
    win = ar[:, None, :] * bb2[None, :, :] + ai[:, None, :] * bb2s[None, :, :]
    win_ref[0] = win.reshape(L * H, LANES).astype(win_ref.dtype)
    ar, ai = powers(tcol + 1.0)
    wo = ar[:, None, :] * c_a[None, :, :] + ai[:, None, :] * c_b[None, :, :]
    wo_ref[0] = wo.reshape(L * H, LANES).astype(wo_ref.dtype)
    ar, ai = powers(tcol)
    wo0 = (ar[:, None, :] * c_a[None, :, :] + ai[:, None, :] * c_b[None, :, :]).reshape(L * H, LANES)
    kt = lax.dot_general(bb2, wo0, (((1,), (1,)), ((), ())),
                         precision=lax.Precision.HIGHEST, preferred_element_type=F32)
    lane = lax.broadcasted_iota(jnp.int32, (H, L * H), 1)
    rows = []
    for s in range(L):
        if s == 0:
            rows.append(kt)
        else:
            rows.append(jnp.where(lane >= s * H, pltpu.roll(kt, s * H, 1), 0.0))
    tt_ref[0] = jnp.concatenate(rows, axis=0).astype(tt_ref.dtype)
    sgn = _half_sign((1, LANES), -1.0, 1.0)
    pr, pi = powers(jnp.full((1, 1), float(L), F32))
    d1, d2 = [], []
    for _ in range(n_dec):
        d1.append(pr)
        d2.append(pi * sgn)
        pr, pi = pr * pr - pi * pi, 2.0 * pr * pi
    dec_ref[0] = jnp.concatenate(d1 + d2, axis=0)


def _shift_rows(x, sh):
    rows = lax.broadcasted_iota(jnp.int32, x.shape, 0)
    return jnp.where(rows >= sh, pltpu.roll(x, sh, 0), 0.0)


def _s5_main_kernel(u_ref, win_ref, wo_ref, tt_ref, dec_ref, d_ref, y_ref, *, bsz, n_dec):
    u = u_ref[0]
    ub = u.astype(BF16)
    nc = u.shape[0] // bsz
    z = jnp.dot(ub, win_ref[0], preferred_element_type=F32)
    dec = dec_ref[0]
    states = []
    for b in range(bsz):
        x = z[b * nc:(b + 1) * nc]
        for k in range(n_dec):
            if (1 << k) >= nc:
                break
            xs = _shift_rows(x, 1 << k)
            x = x + dec[k:k + 1] * xs + dec[n_dec + k:n_dec + k + 1] * pltpu.roll(xs, SSM_STATE, 1)
        states.append(_shift_rows(x, 1))
    s = jnp.concatenate(states, axis=0).astype(BF16)
    y = lax.dot_general(s, wo_ref[0], (((1,), (1,)), ((), ())), preferred_element_type=F32)
    y = y + jnp.dot(ub, tt_ref[0], preferred_element_type=F32)
    y_ref[0] = y + d_ref[0] * u


def s5_scan(prim_chunks, lam_re, lam_im, log_step, b_re, b_im, c_re, c_im, d_skip, *, bsz):
    g, rows, lh = prim_chunks.shape
    L, H, P = S5_CHUNK, SSM_GROUP, SSM_STATE
    nc = rows // bsz
    n_dec = max(1, (nc - 1).bit_length())
    dup = lambda a: jnp.concatenate([a, a], axis=-1).reshape(g, 1, 2 * P)
    lr2 = dup(lam_re)
    li2 = dup(lam_im)
    ls2 = jnp.broadcast_to(log_step[:, None, None], (g, 1, 2 * P))
    bt_re = jnp.swapaxes(b_re, 1, 2)
    bt_im = jnp.swapaxes(b_im, 1, 2)
    b2 = jnp.concatenate([bt_re, bt_im], axis=-1)
    b2s = jnp.concatenate([bt_im, bt_re], axis=-1)
    c2 = jnp.concatenate([c_re, c_im], axis=-1)
    c2s = jnp.concatenate([c_im, c_re], axis=-1)
    vec = pl.BlockSpec((1, 1, 2 * P), lambda i: (i, 0, 0))
    mat = pl.BlockSpec((1, H, 2 * P), lambda i: (i, 0, 0))
    big = pl.BlockSpec((1, L * H, 2 * P), lambda i: (i, 0, 0))
    win, wo, tt, dec = pl.pallas_call(
        functools.partial(_s5_prep_kernel, n_dec=n_dec),
        grid=(g,),
        in_specs=[vec, vec, vec, mat, mat, mat, mat],
        out_specs=[big, big,
                   pl.BlockSpec((1, L * H, L * H), lambda i: (i, 0, 0)),
                   pl.BlockSpec((1, 2 * n_dec, 2 * P), lambda i: (i, 0, 0))],
        out_shape=[jax.ShapeDtypeStruct((g, L * H, 2 * P), BF16),
                   jax.ShapeDtypeStruct((g, L * H, 2 * P), BF16),
                   jax.ShapeDtypeStruct((g, L * H, L * H), BF16),
                   jax.ShapeDtypeStruct((g, 2 * n_dec, 2 * P), F32)],
        compiler_params=_params("parallel"),
        name="s5_prep",
    )(lr2, li2, ls2, b2, b2s, c2, c2s)
    d_tiled = jnp.tile(d_skip, (1, L)).reshape(g, 1, lh)
    return pl.pallas_call(
        functools.partial(_s5_main_kernel, bsz=bsz, n_dec=n_dec),
        grid=(g,),
        in_specs=[pl.BlockSpec((1, rows, lh), lambda i: (i, 0, 0)),
                  big, big,
                  pl.BlockSpec((1, L * H, L * H), lambda i: (i, 0, 0)),
                  pl.BlockSpec((1, 2 * n_dec, 2 * P), lambda i: (i, 0, 0)),
                  pl.BlockSpec((1, 1, lh), lambda i: (i, 0, 0))],
        out_specs=pl.BlockSpec((1, rows, lh), lambda i: (i, 0, 0)),
        out_shape=jax.ShapeDtypeStruct((g, rows, lh), F32),
        compiler_params=_params("parallel"),
        name="s5_main",
    )(prim_chunks, win, wo, tt, dec, d_tiled)


def _glu_kernel(y_ref, w_ref, o_ref):
    y = y_ref[...]
    c = math.sqrt(2.0 / math.pi)
    g = 0.5 * y * (1.0 + jnp.tanh(c * (y + 0.044715 * (y * y * y))))
    z = jnp.dot(g.astype(BF16), w_ref[...].astype(BF16), preferred_element_type=F32)
    o_ref[...] = (g * (1.0 / (1.0 + jnp.exp(-z)))).astype(o_ref.dtype)


def gelu_glu(y, w_glu, tm=512):
    m, n = y.shape
    tm = min(tm, m)
    return pl.pallas_call(
        _glu_kernel,
        grid=(m // tm,),
        in_specs=[pl.BlockSpec((tm, n), lambda i: (i, 0)),
                  pl.BlockSpec((n, n), lambda i: (0, 0))],
        out_specs=pl.BlockSpec((tm, n), lambda i: (i, 0)),
        out_shape=jax.ShapeDtypeStruct((m, n), BF16),
        compiler_params=_params("parallel"),
        name="gelu_glu",
    )(y, w_glu)


def s5_mixer(proj, lam_re, lam_im, log_step, b_re, b_im, c_re, c_im, d_skip, w_glu, *, bsz):
    t = proj.shape[0]
    g = lam_re.shape[0]
    ssm_w = g * SSM_GROUP
    nchunks = t // S5_CHUNK
    u = proj[:, :ssm_w].reshape(nchunks, S5_CHUNK, g, SSM_GROUP)
    u = u.transpose(2, 0, 1, 3).reshape(g, nchunks, S5_CHUNK * SSM_GROUP)
    y = s5_scan(u, lam_re, lam_im, log_step, b_re, b_im, c_re, c_im, d_skip, bsz=bsz)
    y = y.reshape(g, nchunks, S5_CHUNK, SSM_GROUP).transpose(1, 2, 0, 3).reshape(t, ssm_w)
    return gelu_glu(y, w_glu)


def _sb_kernel(q_ref, k_ref, v_ref, o_ref, *, tq, scale):
    i = pl.program_id(2)
    q = q_ref[...].astype(BF16)
    row = lax.broadcasted_iota(jnp.int32, (tq, tq), 0)
    col = lax.broadcasted_iota(jnp.int32, (tq, tq), 1)
    later = jnp.where(row > col, 1.0, 0.0).astype(BF16)
    causal = col < row

    def block(j, carry, diagonal):
        acc, rem = carry
        start = pl.multiple_of(j * tq, tq)
        kj = k_ref[pl.ds(start, tq), :]
        vj = v_ref[pl.ds(start, tq), :]
        z = lax.dot_general(q, kj, (((1,), (1,)), ((), ())),
                            preferred_element_type=F32) * scale
        sp = jnp.maximum(z, 0.0) + jnp.log(1.0 + jnp.exp(-jnp.abs(z)))
        spm = jnp.where(causal, sp, 0.0) if diagonal else sp
        hi = spm.astype(BF16)
        lo = (spm - hi.astype(F32)).astype(BF16)
        cs = (jnp.dot(hi, later, preferred_element_type=F32)
              + jnp.dot(lo, later, preferred_element_type=F32))
        w = jnp.exp(z - sp - cs + rem)
        if diagonal:
            w = jnp.where(causal, w, 0.0)
        acc = acc + jnp.dot(w.astype(BF16), vj, preferred_element_type=F32)
        rem = rem - jnp.sum(spm, axis=1, keepdims=True)
        return acc, rem

    carry = (jnp.zeros((tq, q_ref.shape[1]), F32), jnp.zeros((tq, 1), F32))
    carry = block(i, carry, True)
    carry = lax.fori_loop(0, i, lambda n, c: block(i - 1 - n, c, False), carry)
    o_ref[...] = carry[0].astype(o_ref.dtype)


def stick_breaking(proj, kv, *, bsz, heads, tq=256):
    t = proj.shape[0]
    seq = t // bsz
    hd = SB_HEAD_DIM
    tq = min(tq, seq)
    nq = seq // tq
    return pl.pallas_call(
        functools.partial(_sb_kernel, tq=tq, scale=1.0 / math.sqrt(hd)),
        grid=(bsz, heads, nq),
        in_specs=[pl.BlockSpec((tq, hd), lambda b, h, i: (b * nq + i, h)),
                  pl.BlockSpec((seq, hd), lambda b, h, i: (b, h)),
                  pl.BlockSpec((seq, hd), lambda b, h, i: (b, heads + h))],
        out_specs=pl.BlockSpec((tq, hd), lambda b, h, i: (b * nq + i, h)),
        out_shape=jax.ShapeDtypeStruct((t, heads * hd), BF16),
        compiler_params=_params("parallel", "parallel", "arbitrary"),
        name="stick_breaking",
    )(proj, kv, kv)


def _ffn_up_kernel(x_ref, wa_ref, wb_ref, cwa_ref, cwb_ref, cba_ref, cbb_ref, o_ref,
                   bufa_ref, bufb_ref, *, tm, tiles_per_seq):
    i = pl.program_id(1)
    x = x_ref[...]

    @pl.when(i % tiles_per_seq == 0)
    def _():
        bufa_ref[0:SUBLANES, :] = jnp.zeros((SUBLANES, bufa_ref.shape[1]), F32)
        bufb_ref[0:SUBLANES, :] = jnp.zeros((SUBLANES, bufb_ref.shape[1]), F32)

    def conv(buf_ref, w_ref, cw_ref, cb_ref):
        up = jnp.dot(x, w_ref[...].astype(BF16), preferred_element_type=F32)
        buf_ref[SUBLANES:SUBLANES + tm, :] = up
        y = (buf_ref[SUBLANES - 2:SUBLANES - 2 + tm, :] * cw_ref[0:1, :]
             + buf_ref[SUBLANES - 1:SUBLANES - 1 + tm, :] * cw_ref[1:2, :]
             + up * cw_ref[2:3, :] + cb_ref[...])
        buf_ref[0:SUBLANES, :] = buf_ref[tm:tm + SUBLANES, :]
        return y

    a = conv(bufa_ref, wa_ref, cwa_ref, cba_ref)
    b = conv(bufb_ref, wb_ref, cwb_ref, cbb_ref)
    o_ref[...] = (a * (1.0 / (1.0 + jnp.exp(-a))) * b).astype(o_ref.dtype)


def ffn_up(xn, w_up, conv_w, conv_b, *, seq, tm=1024, tf=256):
    t, d = xn.shape
    ff = w_up.shape[1] // 2
    tm = min(tm, seq)
    nf = ff // tf
    cb = conv_b.reshape(1, 2 * ff)
    return pl.pallas_call(
        functools.partial(_ffn_up_kernel, tm=tm, tiles_per_seq=seq // tm),
        grid=(nf, t // tm),
        in_specs=[pl.BlockSpec((tm, d), lambda j, i: (i, 0)),
                  pl.BlockSpec((d, tf), lambda j, i: (0, j)),
                  pl.BlockSpec((d, tf), lambda j, i: (0, nf + j)),
                  pl.BlockSpec((CONV_W, tf), lambda j, i: (0, j)),
                  pl.BlockSpec((CONV_W, tf), lambda j, i: (0, nf + j)),
                  pl.BlockSpec((1, tf), lambda j, i: (0, j)),
                  pl.BlockSpec((1, tf), lambda j, i: (0, nf + j))],
        out_specs=pl.BlockSpec((tm, tf), lambda j, i: (i, j)),
        out_shape=jax.ShapeDtypeStruct((t, ff), BF16),
        scratch_shapes=[pltpu.VMEM((tm + SUBLANES, tf), F32),
                        pltpu.VMEM((tm + SUBLANES, tf), F32)],
        compiler_params=_params("arbitrary", "arbitrary"),
        name="ffn_up",
    )(xn, w_up, w_up, conv_w, conv_w, cb, cb)


def conv_ffn(h, g, w_up, conv_w, conv_b, w_down, *, seq):
    act = ffn_up(rmsnorm(h, g), w_up, conv_w, conv_b, seq=seq)
    ff = w_down.shape[0]
    tk = ff // 2 if (ff // 2) % LANES == 0 else None
    return matmul([act], w_down.astype(BF16), res=h, tk=tk, name="ffn_down")


def kernel(x, mem, norm_mix_g, w_in, w_out, mem_norm_g, w_mem_kv, mem_q_norm_g, mem_k_norm_g,
           s5_lam_re, s5_lam_im, s5_log_step, s5_b_re, s5_b_im, s5_c_re, s5_c_im, s5_d,
           s5_w_glu, kv_norm_g, w_kv_shared, norm_ffn_g, w_ffn_up, ffn_conv_w, ffn_conv_b,
           w_ffn_down):
    bsz, seq, d = x.shape
    depth = w_in.shape[0]
    n_a = s5_lam_re.shape[0]
    mem2 = mem.reshape(bsz * mem.shape[1], d)
    ssm_w = s5_w_glu.shape[1]
    mem_w = w_mem_kv.shape[2] // 2
    h = x.reshape(bsz * seq, d)
    kv_sh = None
    for i in range(depth):
        proj = matmul([rmsnorm(h, norm_mix_g[i])], w_in[i], name="w_in")
        kv_mem = matmul([rmsnorm(mem2, mem_norm_g[i])], w_mem_kv[i], name="w_mem_kv")
        m_out = mem_attention(proj, kv_mem, mem_q_norm_g[i], mem_k_norm_g[i],
                              bsz=bsz, q_col_block=ssm_w // mem_w)
        if i < n_a:
            p_out = s5_mixer(proj, s5_lam_re[i], s5_lam_im[i], s5_log_step[i], s5_b_re[i],
                             s5_b_im[i], s5_c_re[i], s5_c_im[i], s5_d[i], s5_w_glu[i], bsz=bsz)
        else:
            p_out = stick_breaking(proj, kv_sh, bsz=bsz, heads=ssm_w // SB_HEAD_DIM)
        h = matmul([p_out, m_out], w_out[i], res=h, name="w_out")
        h = conv_ffn(h, norm_ffn_g[i], w_ffn_up[i], ffn_conv_w[i], ffn_conv_b[i],
                     w_ffn_down[i], seq=seq)
        if i == n_a - 1:
            kv_sh = matmul([rmsnorm(h, kv_norm_g)], w_kv_shared, out_dtype=BF16,
                           name="w_kv_shared")
    return h.reshape(bsz, seq, d)
```

```python
import functools
import math

import jax
import jax.numpy as jnp
from jax import lax
from jax.experimental import pallas as pl
from jax.experimental.pallas import tpu as pltpu

F32 = jnp.float32
BF16 = jnp.bfloat16
EPS = 1e-6

V7X_VMEM_BYTES = 64 * 1024 * 1024
VMEM_LIMIT_BYTES = V7X_VMEM_BYTES - 8 * 1024 * 1024
SUBLANES = 8
LANES = 128

SSM_GROUP = 16
SSM_STATE = 64
S5_CHUNK = 16
S5_BLOCK_GROUPS = LANES // SSM_GROUP
SB_HEAD_DIM = 128
MEM_HEADS = 4
CONV_W = 3
F32_EXP_UNDERFLOW = 104.0


def _params(*semantics):
    return pltpu.CompilerParams(dimension_semantics=semantics,
                                vmem_limit_bytes=VMEM_LIMIT_BYTES)


def _rms_kernel(x_ref, g_ref, o_ref):
    x = x_ref[...]
    ms = jnp.mean(x * x, axis=-1, keepdims=True)
    o_ref[...] = (x * lax.rsqrt(ms + EPS) * g_ref[...]).astype(o_ref.dtype)


def rmsnorm(x, g, tm=256):
    m, d = x.shape
    tm = min(tm, m)
    return pl.pallas_call(
        _rms_kernel,
        grid=(m // tm,),
        in_specs=[pl.BlockSpec((tm, d), lambda i: (i, 0)),
                  pl.BlockSpec((1, d), lambda i: (0, 0))],
        out_specs=pl.BlockSpec((tm, d), lambda i: (i, 0)),
        out_shape=jax.ShapeDtypeStruct((m, d), BF16),
        compiler_params=_params("parallel"),
        name="rmsnorm",
    )(x, g.reshape(1, d))


def _mm_kernel(*refs, n_lhs, has_res, nk):
    a_refs = refs[:n_lhs]
    w_refs = refs[n_lhs:2 * n_lhs]
    r_ref = refs[2 * n_lhs] if has_res else None
    o_ref = refs[-1]
    acc = None
    for a_ref, w_ref in zip(a_refs, w_refs):
        d = jnp.dot(a_ref[...], w_ref[...].astype(BF16), preferred_element_type=F32)
        acc = d if acc is None else acc + d
    if nk == 1:
        if has_res:
            acc = r_ref[...] + acc
        o_ref[...] = acc.astype(o_ref.dtype)
    else:
        k = pl.program_id(2)

        @pl.when(k == 0)
        def _():
            o_ref[...] = (r_ref[...] + acc) if has_res else acc

        @pl.when(k > 0)
        def _():
            o_ref[...] += acc


def matmul(lhs, w, layer, res=None, out_dtype=F32, tm=1024, tn=512, tk=None, name="matmul"):
    m = lhs[0].shape[0]
    n = w.shape[2]
    tm = min(tm, m)
    tn = min(tn, n)
    n_lhs = len(lhs)
    in_specs, args = [], []
    if tk is None:
        nk = 1
        for a in lhs:
            in_specs.append(pl.BlockSpec((tm, a.shape[1]), lambda j, i: (i, 0)))
            args.append(a)
        row = 0
        for a in lhs:
            ka = a.shape[1]
            assert row % ka == 0
            in_specs.append(pl.BlockSpec((None, ka, tn), lambda j, i, b=row // ka: (layer, b, j)))
            args.append(w)
            row += ka
        assert row == w.shape[1]
        if res is not None:
            in_specs.append(pl.BlockSpec((tm, tn), lambda j, i: (i, j)))
            args.append(res)
        out_spec = pl.BlockSpec((tm, tn), lambda j, i: (i, j))
        grid = (n // tn, m // tm)
        sem = ("parallel", "parallel")
    else:
        assert n_lhs == 1 and out_dtype == F32
        nk = w.shape[1] // tk
        in_specs.append(pl.BlockSpec((tm, tk), lambda j, i, k: (i, k)))
        args.append(lhs[0])
        in_specs.append(pl.BlockSpec((None, tk, tn), lambda j, i, k: (layer, k, j)))
        args.append(w)
        if res is not None:
            in_specs.append(pl.BlockSpec((tm, tn), lambda j, i, k: (i, j)))
            args.append(res)
        out_spec = pl.BlockSpec((tm, tn), lambda j, i, k: (i, j))
        grid = (n // tn, m // tm, nk)
        sem = ("parallel", "parallel", "arbitrary")
    return pl.pallas_call(
        functools.partial(_mm_kernel, n_lhs=n_lhs, has_res=res is not None, nk=nk),
        grid=grid,
        in_specs=in_specs,
        out_specs=out_spec,
        out_shape=jax.ShapeDtypeStruct((m, n), out_dtype),
        compiler_params=_params(*sem),
        name=name,
    )(*args)


def _mem_attn_kernel(q_ref, k_ref, v_ref, gq_ref, gk_ref, o_ref, *, heads, hd):
    gq = gq_ref[...]
    gk = gk_ref[...]
    inv_sqrt = 1.0 / math.sqrt(hd)
    for h in range(heads):
        sl = slice(h * hd, (h + 1) * hd)
        q = q_ref[:, sl]
        k = k_ref[:, sl]
        qn = (q * lax.rsqrt(jnp.mean(q * q, axis=-1, keepdims=True) + EPS) * gq).astype(BF16)
        kn = (k * lax.rsqrt(jnp.mean(k * k, axis=-1, keepdims=True) + EPS) * gk).astype(BF16)
        logits = lax.dot_general(qn, kn, (((1,), (1,)), ((), ())),
                                 preferred_element_type=F32) * inv_sqrt
        mx = jnp.max(logits, axis=-1, keepdims=True)
        p = jnp.exp(logits - mx)
        p = p / jnp.sum(p, axis=-1, keepdims=True)
        o = jnp.dot(p.astype(BF16), v_ref[:, sl].astype(BF16), preferred_element_type=F32)
        o_ref[:, sl] = o.astype(o_ref.dtype)


def mem_attention(proj, kv_mem, gq, gk, *, bsz, q_col_block, ts=512):
    t = proj.shape[0]
    seq = t // bsz
    mem_w = kv_mem.shape[1] // 2
    mem_tokens = kv_mem.shape[0] // bsz
    hd = mem_w // MEM_HEADS
    ts = min(ts, seq)
    nt = seq // ts
    return pl.pallas_call(
        functools.partial(_mem_attn_kernel, heads=MEM_HEADS, hd=hd),
        grid=(bsz, nt),
        in_specs=[pl.BlockSpec((ts, mem_w), lambda b, i: (b * nt + i, q_col_block)),
                  pl.BlockSpec((mem_tokens, mem_w), lambda b, i: (b, 0)),
                  pl.BlockSpec((mem_tokens, mem_w), lambda b, i: (b, 1)),
                  pl.BlockSpec((1, hd), lambda b, i: (0, 0)),
                  pl.BlockSpec((1, hd), lambda b, i: (0, 0))],
        out_specs=pl.BlockSpec((ts, mem_w), lambda b, i: (b * nt + i, 0)),
        out_shape=jax.ShapeDtypeStruct((t, mem_w), BF16),
        compiler_params=_params("parallel", "parallel"),
        name="mem_attention",
    )(proj, kv_mem, kv_mem, gq.reshape(1, hd), gk.reshape(1, hd))


def _half_sign(shape, first, second):
    lane = lax.broadcasted_iota(jnp.int32, shape, len(shape) - 1)
    return jnp.where(lane < SSM_STATE, first, second).astype(F32)


def _swap_halves(x):
    return jnp.concatenate([x[:, SSM_STATE:], x[:, :SSM_STATE]], axis=1)


def _s5_prep_kernel(lr_ref, li_ref, ls_ref, b2_ref, c2_ref,
                    wc_ref, woc_ref, dd_ref, dec_ref, *, n_dec):
    L = S5_CHUNK
    lr = jnp.minimum(lr_ref[...], -1e-4)
    li = li_ref[...]
    step = jnp.exp(ls_ref[...])
    mag = jnp.exp(lr * step)
    a_re = mag * jnp.cos(li * step)
    a_im = mag * jnp.sin(li * step)
    p_re = a_re - 1.0
    den = lr * lr + li * li
    f_re = (p_re * lr + a_im * li) / den
    f_im = (a_im * lr - p_re * li) / den
    shape = lr.shape
    b2 = b2_ref[...]
    bb2 = f_re * b2 + f_im * (_swap_halves(b2) * _half_sign(shape, -1.0, 1.0))
    bb2s = _swap_halves(bb2) * _half_sign(shape, -1.0, 1.0)
    c2 = c2_ref[...]
    c_a = c2 * _half_sign(shape, 1.0, -1.0)
    c_b = -_swap_halves(c2)
    pw = [(jnp.ones(shape, F32), jnp.zeros(shape, F32))]
    for _ in range(L):
        pr, pi = pw[-1]
        pw.append((pr * a_re - pi * a_im, pr * a_im + pi * a_re))
    row_g = lax.broadcasted_iota(jnp.int32, shape, 0) // SSM_GROUP
    col_g = lax.broadcasted_iota(jnp.int32, shape, 1) // SSM_GROUP
    same_group = row_g == col_g
    for n in range(L):
        ar, ai = pw[L - 1 - n]
        wc_ref[0, n] = (ar * bb2 + ai * bb2s).astype(wc_ref.dtype)
        ar, ai = pw[n + 1]
        woc_ref[0, n] = (ar * c_a + ai * c_b).astype(woc_ref.dtype)
        ar, ai = pw[n]
        w0 = ar * c_a + ai * c_b
        kk = lax.dot_general(bb2, w0, (((1,), (1,)), ((), ())),
                             precision=lax.Precision.HIGHEST, preferred_element_type=F32)
        dd_ref[0, n] = jnp.where(same_group, kk, 0.0).astype(dd_ref.dtype)
    sgn = _half_sign(shape, -1.0, 1.0)
    pr, pi = pw[L]
    for k in range(n_dec):
        dec_ref[0, k] = pr
        dec_ref[0, n_dec + k] = pi * sgn
        pr, pi = pr * pr - pi * pi, 2.0 * pr * pi


def _shift_rows(x, sh):
    rows = lax.broadcasted_iota(jnp.int32, x.shape, 0)
    return jnp.where(rows >= sh, pltpu.roll(x, sh, 0), 0.0)


def _s5_main_kernel(u_ref, wc_ref, woc_ref, dd_ref, dec_ref, d_ref, y_ref,
                    win_s, wot_s, tt_s, *, bsz, seq, n_dec):
    L, H, GB = S5_CHUNK, SSM_GROUP, S5_BLOCK_GROUPS
    nc = seq // L

    @pl.when(pl.program_id(0) == 0)
    def _():
        win_s[...] = jnp.zeros(win_s.shape, win_s.dtype)
        wot_s[...] = jnp.zeros(wot_s.shape, wot_s.dtype)
        tt_s[...] = jnp.zeros(tt_s.shape, tt_s.dtype)

    for n in range(L):
        for g in range(GB):
            rows = slice(n * LANES + g * H, n * LANES + (g + 1) * H)
            cols = slice(g * LANES, (g + 1) * LANES)
            win_s[rows, cols] = wc_ref[0, n, g * H:(g + 1) * H, :]
            wot_s[rows, cols] = woc_ref[0, n, g * H:(g + 1) * H, :]
        for t in range(n, L):
            tt_s[n * LANES:(n + 1) * LANES, t * LANES:(t + 1) * LANES] = dd_ref[0, t - n]

    def piece(b, t):
        return u_ref[pl.ds(b * seq + t, nc, stride=L), :]

    ub = jnp.concatenate(
        [jnp.concatenate([piece(b, t).astype(BF16) for t in range(L)], axis=1)
         for b in range(bsz)], axis=0)
    z = jnp.dot(ub, win_s[...], preferred_element_type=F32)
    states = []
    for b in range(bsz):
        per_group = []
        for g in range(GB):
            x = z[b * nc:(b + 1) * nc, g * LANES:(g + 1) * LANES]
            for k in range(n_dec):
                if (1 << k) >= nc:
                    break
                xs = _shift_rows(x, 1 << k)
                a1 = dec_ref[0, k, g * H:g * H + 1, :]
                a2 = dec_ref[0, n_dec + k, g * H:g * H + 1, :]
                x = x + a1 * xs + a2 * pltpu.roll(xs, SSM_STATE, 1)
            per_group.append(_shift_rows(x, 1))
        states.append(jnp.concatenate(per_group, axis=1))
    s = jnp.concatenate(states, axis=0).astype(BF16)
    y = lax.dot_general(s, wot_s[...], (((1,), (1,)), ((), ())), preferred_element_type=F32)
    d_row = d_ref[...]
    tile_pairs = L * LANES // 256
    for j in range(tile_pairs):
        kk = (j + 1) * 256
        yj = y[:, j * 256:(j + 1) * 256] + jnp.dot(
            ub[:, :kk], tt_s[0:kk, j * 256:(j + 1) * 256], preferred_element_type=F32)
        for b in range(bsz):
            for tt in range(2):
                t = 2 * j + tt
                y_ref[pl.ds(b * seq + t, nc, stride=L), :] = (
                    yj[b * nc:(b + 1) * nc, tt * LANES:(tt + 1) * LANES] + d_row * piece(b, t))


def s5_scan(proj, lam_re, lam_im, log_step, b_re, b_im, c_re, c_im, d_skip, *, bsz):
    t = proj.shape[0]
    seq = t // bsz
    g = lam_re.shape[0]
    L, H, P, GB = S5_CHUNK, SSM_GROUP, SSM_STATE, S5_BLOCK_GROUPS
    nb = g // GB
    nc = seq // L
    n_dec = max(1, (nc - 1).bit_length())
    rows = lambda a: jnp.repeat(jnp.concatenate([a, a], axis=-1), H, axis=0)
    lr = rows(lam_re)
    li = rows(lam_im)
    ls = jnp.repeat(jnp.broadcast_to(log_step[:, None], (g, 2 * P)), H, axis=0)
    b2 = jnp.concatenate([jnp.swapaxes(b_re, 1, 2), jnp.swapaxes(b_im, 1, 2)],
                         axis=-1).reshape(g * H, 2 * P)
    c2 = jnp.concatenate([c_re, c_im], axis=-1).reshape(g * H, 2 * P)
    blk = pl.BlockSpec((LANES, 2 * P), lambda i: (i, 0))
    tab = pl.BlockSpec((1, L, LANES, LANES), lambda i: (i, 0, 0, 0))
    dec_spec = pl.BlockSpec((1, 2 * n_dec, LANES, LANES), lambda i: (i, 0, 0, 0))
    wc, woc, dd, dec = pl.pallas_call(
        functools.partial(_s5_prep_kernel, n_dec=n_dec),
        grid=(nb,),
        in_specs=[blk, blk, blk, blk, blk],
        out_specs=[tab, tab, tab, dec_spec],
        out_shape=[jax.ShapeDtypeStruct((nb, L, LANES, LANES), BF16),
                   jax.ShapeDtypeStruct((nb, L, LANES, LANES), BF16),
                   jax.ShapeDtypeStruct((nb, L, LANES, LANES), BF16),
                   jax.ShapeDtypeStruct((nb, 2 * n_dec, LANES, LANES), F32)],
        compiler_params=_params("parallel"),
        name="s5_prep",
    )(lr, li, ls, b2, c2)
    return pl.pallas_call(
        functools.partial(_s5_main_kernel, bsz=bsz, seq=seq, n_dec=n_dec),
        grid=(nb,),
        in_specs=[pl.BlockSpec((t, LANES), lambda i: (0, i)),
                  tab, tab, tab, dec_spec,
                  pl.BlockSpec((1, LANES), lambda i: (0, i))],
        out_specs=pl.BlockSpec((t, LANES), lambda i: (0, i)),
        out_shape=jax.ShapeDtypeStruct((t, g * H), F32),
        scratch_shapes=[pltpu.VMEM((L * LANES, GB * LANES), BF16),
                        pltpu.VMEM((L * LANES, GB * LANES), BF16),
                        pltpu.VMEM((L * LANES, L * LANES), BF16)],
        compiler_params=_params("arbitrary"),
        name="s5_main",
    )(proj, wc, woc, dd, dec, d_skip.reshape(1, g * H))


def _glu_kernel(y_ref, w_ref, o_ref):
    y = y_ref[...]
    c = math.sqrt(2.0 / math.pi)
    g = 0.5 * y * (1.0 + jnp.tanh(c * (y + 0.044715 * (y * y * y))))
    z = jnp.dot(g.astype(BF16), w_ref[...].astype(BF16), preferred_element_type=F32)
    o_ref[...] = (g * (1.0 / (1.0 + jnp.exp(-z)))).astype(o_ref.dtype)


def gelu_glu(y, w_glu, layer, tm=512):
    m, n = y.shape
    tm = min(tm, m)
    return pl.pallas_call(
        _glu_kernel,
        grid=(m // tm,),
        in_specs=[pl.BlockSpec((tm, n), lambda i: (i, 0)),
                  pl.BlockSpec((None, n, n), lambda i: (layer, 0, 0))],
        out_specs=pl.BlockSpec((tm, n), lambda i: (i, 0)),
        out_shape=jax.ShapeDtypeStruct((m, n), BF16),
        compiler_params=_params("parallel"),
        name="gelu_glu",
    )(y, w_glu)


def _sb_kernel(q_ref, k_ref, v_ref, o_ref, *, tq, scale):
    i = pl.program_id(2)
    q = q_ref[...].astype(BF16)
    row = lax.broadcasted_iota(jnp.int32, (tq, tq), 0)
    col = lax.broadcasted_iota(jnp.int32, (tq, tq), 1)
    later = jnp.where(row > col, 1.0, 0.0).astype(BF16)
    causal = col < row

    def block(j, carry, diagonal):
        acc, rem = carry
        start = pl.multiple_of(j * tq, tq)
        kj = k_ref[pl.ds(start, tq), :]
        vj = v_ref[pl.ds(start, tq), :]
        z = lax.dot_general(q, kj, (((1,), (1,)), ((), ())),
                            preferred_element_type=F32) * scale
        sp = jnp.maximum(z, 0.0) + jnp.log(1.0 + jnp.exp(-jnp.abs(z)))
        spm = jnp.where(causal, sp, 0.0) if diagonal else sp
        hi = spm.astype(BF16)
        lo = (spm - hi.astype(F32)).astype(BF16)
        cs = (jnp.dot(hi, later, preferred_element_type=F32)
              + jnp.dot(lo, later, preferred_element_type=F32))
        w = jnp.exp(z - sp - cs + rem)
        if diagonal:
            w = jnp.where(causal, w, 0.0)
        acc = acc + jnp.dot(w.astype(BF16), vj, preferred_element_type=F32)
        rem = rem - jnp.sum(spm, axis=1, keepdims=True)
        return acc, rem

    init = (jnp.zeros((tq, q_ref.shape[1]), F32), jnp.zeros((tq, 1), F32))

    def diagonal_only():
        return block(i, init, True)

    def diagonal_and_previous():
        return block(i - 1, block(i, init, True), False)

    acc, rem = lax.cond(i > 0, diagonal_and_previous, diagonal_only)

    def more(c):
        n, _, rem = c
        return jnp.logical_and(n < i, jnp.max(rem) > -F32_EXP_UNDERFLOW)

    def step(c):
        n, acc, rem = c
        acc, rem = block(i - 1 - n, (acc, rem), False)
        return n + 1, acc, rem

    _, acc, _ = lax.while_loop(more, step, (jnp.int32(1), acc, rem))
    o_ref[...] = acc.astype(o_ref.dtype)


def stick_breaking(proj, kv, *, bsz, heads, tq=256):
    t = proj.shape[0]
    seq = t // bsz
    hd = SB_HEAD_DIM
    tq = min(tq, seq)
    nq = seq // tq
    return pl.pallas_call(
        functools.partial(_sb_kernel, tq=tq, scale=1.0 / math.sqrt(hd)),
        grid=(bsz, heads, nq),
        in_specs=[pl.BlockSpec((tq, hd), lambda b, h, i: (b * nq + i, h)),
                  pl.BlockSpec((seq, hd), lambda b, h, i: (b, h)),
                  pl.BlockSpec((seq, hd), lambda b, h, i: (b, heads + h))],
        out_specs=pl.BlockSpec((tq, hd), lambda b, h, i: (b * nq + i, h)),
        out_shape=jax.ShapeDtypeStruct((t, heads * hd), BF16),
        compiler_params=_params("parallel", "parallel", "arbitrary"),
        name="stick_breaking",
    )(proj, kv, kv)


def _ffn_up_kernel(x_ref, wa_ref, wb_ref, cwa_ref, cwb_ref, cba_ref, cbb_ref, o_ref,
                   bufa_ref, bufb_ref, *, tm, sub, tiles_per_seq):
    i = pl.program_id(1)

    @pl.when(i % tiles_per_seq == 0)
    def _():
        bufa_ref[0:SUBLANES, :] = jnp.zeros((SUBLANES, bufa_ref.shape[1]), F32)
        bufb_ref[0:SUBLANES, :] = jnp.zeros((SUBLANES, bufb_ref.shape[1]), F32)

    wa = wa_ref[...].astype(BF16)
    wb = wb_ref[...].astype(BF16)

    def conv(x, r0, buf_ref, w, cw_ref, cb_ref):
        up = jnp.dot(x, w, preferred_element_type=F32)
        base = SUBLANES + r0
        buf_ref[base:base + sub, :] = up
        return (buf_ref[base - 2:base - 2 + sub, :] * cw_ref[0:1, :]
                + buf_ref[base - 1:base - 1 + sub, :] * cw_ref[1:2, :]
                + up * cw_ref[2:3, :] + cb_ref[...])

    for r0 in range(0, tm, sub):
        x = x_ref[r0:r0 + sub, :]
        a = conv(x, r0, bufa_ref, wa, cwa_ref, cba_ref)
        b = conv(x, r0, bufb_ref, wb, cwb_ref, cbb_ref)
        o_ref[r0:r0 + sub, :] = (a * (1.0 / (1.0 + jnp.exp(-a))) * b).astype(o_ref.dtype)
    bufa_ref[0:SUBLANES, :] = bufa_ref[tm:tm + SUBLANES, :]
    bufb_ref[0:SUBLANES, :] = bufb_ref[tm:tm + SUBLANES, :]


def ffn_up(xn, w_up, conv_w, conv_b, layer, *, seq, tm=1024, tf=256, sub=256):
    t, d = xn.shape
    ff = w_up.shape[2] // 2
    tm = min(tm, seq)
    sub = min(sub, tm)
    nf = ff // tf
    cb = conv_b.reshape(conv_b.shape[0], 1, 2 * ff)
    return pl.pallas_call(
        functools.partial(_ffn_up_kernel, tm=tm, sub=sub, tiles_per_seq=seq // tm),
        grid=(nf, t // tm),
        in_specs=[pl.BlockSpec((tm, d), lambda j, i: (i, 0)),
                  pl.BlockSpec((None, d, tf), lambda j, i: (layer, 0, j)),
                  pl.BlockSpec((None, d, tf), lambda j, i: (layer, 0, nf + j)),
                  pl.BlockSpec((None, CONV_W, tf), lambda j, i: (layer, 0, j)),
                  pl.BlockSpec((None, CONV_W, tf), lambda j, i: (layer, 0, nf + j)),
                  pl.BlockSpec((None, 1, tf), lambda j, i: (layer, 0, j)),
                  pl.BlockSpec((None, 1, tf), lambda j, i: (layer, 0, nf + j))],
        out_specs=pl.BlockSpec((tm, tf), lambda j, i: (i, j)),
        out_shape=jax.ShapeDtypeStruct((t, ff), BF16),
        scratch_shapes=[pltpu.VMEM((tm + SUBLANES, tf), F32),
                        pltpu.VMEM((tm + SUBLANES, tf), F32)],
        compiler_params=_params("arbitrary", "arbitrary"),
        name="ffn_up",
    )(xn, w_up, w_up, conv_w, conv_w, cb, cb)


def kernel(x, mem, norm_mix_g, w_in, w_out, mem_norm_g, w_mem_kv, mem_q_norm_g, mem_k_norm_g,
           s5_lam_re, s5_lam_im, s5_log_step, s5_b_re, s5_b_im, s5_c_re, s5_c_im, s5_d,
           s5_w_glu, kv_norm_g, w_kv_shared, norm_ffn_g, w_ffn_up, ffn_conv_w, ffn_conv_b,
           w_ffn_down):
    bsz, seq, d = x.shape
    depth = w_in.shape[0]
    n_a = s5_lam_re.shape[0]
    mem2 = mem.reshape(bsz * mem.shape[1], d)
    ssm_w = s5_w_glu.shape[1]
    mem_w = w_mem_kv.shape[2] // 2
    ff = w_ffn_down.shape[1]
    w_down = w_ffn_down.astype(BF16)
    h = x.reshape(bsz * seq, d)
    kv_sh = None
    for i in range(depth):
        proj = matmul([rmsnorm(h, norm_mix_g[i])], w_in, i, name="w_in")
        kv_mem = matmul([rmsnorm(mem2, mem_norm_g[i])], w_mem_kv, i, name="w_mem_kv")
        m_out = mem_attention(proj, kv_mem, mem_q_norm_g[i], mem_k_norm_g[i],
                              bsz=bsz, q_col_block=ssm_w // mem_w)
        if i < n_a:
            y = s5_scan(proj, s5_lam_re[i], s5_lam_im[i], s5_log_step[i], s5_b_re[i],
                        s5_b_im[i], s5_c_re[i], s5_c_im[i], s5_d[i], bsz=bsz)
            p_out = gelu_glu(y, s5_w_glu, i)
        else:
            p_out = stick_breaking(proj, kv_sh, bsz=bsz, heads=ssm_w // SB_HEAD_DIM)
        h = matmul([p_out, m_out], w_out, i, res=h, name="w_out")
        act = ffn_up(rmsnorm(h, norm_ffn_g[i]), w_ffn_up, ffn_conv_w, ffn_conv_b, i, seq=seq)
        h = matmul([act], w_down, i, res=h, tk=ff // 2, name="ffn_down")
        if i == n_a - 1:
            kv_sh = matmul([rmsnorm(h, kv_norm_g)], w_kv_shared[None], 0, out_dtype=BF16,
                           name="w_kv_shared")
    return h.reshape(bsz, seq, d)
```

```python
import functools
import math

import jax
import jax.numpy as jnp
from jax import lax
from jax.experimental import pallas as pl
from jax.experimental.pallas import tpu as pltpu

F32 = jnp.float32
BF16 = jnp.bfloat16
U32 = jnp.uint32
EPS = 1e-6

V7X_VMEM_BYTES = 64 * 1024 * 1024
VMEM_LIMIT_BYTES = V7X_VMEM_BYTES - 8 * 1024 * 1024
SUBLANES = 8
LANES = 128

SSM_GROUP = 16
SSM_STATE = 64
S5_CHUNK = 16
S5_BLOCK_GROUPS = LANES // SSM_GROUP
SB_HEAD_DIM = 128
MEM_HEADS = 4
CONV_W = 3
F32_EXP_UNDERFLOW = 104.0


def _params(*semantics):
    return pltpu.CompilerParams(dimension_semantics=semantics,
                                vmem_limit_bytes=VMEM_LIMIT_BYTES)


def _pack_rows(y):
    return pltpu.bitcast(y.astype(BF16), U32)


def _unpack_rows(p):
    return pltpu.bitcast(p, BF16)


def _rms_kernel(x_ref, g_ref, o_ref):
    x = x_ref[...]
    ms = jnp.mean(x * x, axis=-1, keepdims=True)
    o_ref[...] = _pack_rows(x * lax.rsqrt(ms + EPS) * g_ref[...])


def rmsnorm(x, g, tm=256):
    m, d = x.shape
    tm = min(tm, m)
    return pl.pallas_call(
        _rms_kernel,
        grid=(m // tm,),
        in_specs=[pl.BlockSpec((tm, d), lambda i: (i, 0)),
                  pl.BlockSpec((1, d), lambda i: (0, 0))],
        out_specs=pl.BlockSpec((tm // 2, d), lambda i: (i, 0)),
        out_shape=jax.ShapeDtypeStruct((m // 2, d), U32),
        compiler_params=_params("parallel"),
        name="rmsnorm",
    )(x, g.reshape(1, d))


def _mm_kernel(*refs, n_lhs, has_res):
    a_refs = refs[:n_lhs]
    w_refs = refs[n_lhs:2 * n_lhs]
    r_ref = refs[2 * n_lhs] if has_res else None
    o_ref = refs[-1]
    acc = None
    for a_ref, w_ref in zip(a_refs, w_refs):
        d = jnp.dot(_unpack_rows(a_ref[...]), w_ref[...].astype(BF16),
                    preferred_element_type=F32)
        acc = d if acc is None else acc + d
    if has_res:
        acc = r_ref[...] + acc
    o_ref[...] = acc.astype(o_ref.dtype)


def matmul(lhs, w, layer, res=None, out_dtype=F32, tm=1024, tn=512, name="matmul"):
    m = 2 * lhs[0].shape[0]
    n = w.shape[2]
    tm = min(tm, m)
    tn = min(tn, n)
    in_specs, args = [], []
    for a in lhs:
        in_specs.append(pl.BlockSpec((tm // 2, a.shape[1]), lambda j, i: (i, 0)))
        args.append(a)
    row = 0
    for a in lhs:
        ka = a.shape[1]
        assert row % ka == 0
        in_specs.append(pl.BlockSpec((None, ka, tn), lambda j, i, b=row // ka: (layer, b, j)))
        args.append(w)
        row += ka
    assert row == w.shape[1]
    if res is not None:
        in_specs.append(pl.BlockSpec((tm, tn), lambda j, i: (i, j)))
        args.append(res)
    return pl.pallas_call(
        functools.partial(_mm_kernel, n_lhs=len(lhs), has_res=res is not None),
        grid=(n // tn, m // tm),
        in_specs=in_specs,
        out_specs=pl.BlockSpec((tm, tn), lambda j, i: (i, j)),
        out_shape=jax.ShapeDtypeStruct((m, n), out_dtype),
        compiler_params=_params("parallel", "parallel"),
        name=name,
    )(*args)


def _mem_attn_kernel(q_ref, k_ref, v_ref, gq_ref, gk_ref, o_ref, *, heads, hd):
    gq = gq_ref[...]
    gk = gk_ref[...]
    inv_sqrt = 1.0 / math.sqrt(hd)
    for h in range(heads):
        sl = slice(h * hd, (h + 1) * hd)
        q = q_ref[:, sl]
        k = k_ref[:, sl]
        qn = (q * lax.rsqrt(jnp.mean(q * q, axis=-1, keepdims=True) + EPS) * gq).astype(BF16)
        kn = (k * lax.rsqrt(jnp.mean(k * k, axis=-1, keepdims=True) + EPS) * gk).astype(BF16)
        logits = lax.dot_general(qn, kn, (((1,), (1,)), ((), ())),
                                 preferred_element_type=F32) * inv_sqrt
        mx = jnp.max(logits, axis=-1, keepdims=True)
        p = jnp.exp(logits - mx)
        p = p / jnp.sum(p, axis=-1, keepdims=True)
        o = jnp.dot(p.astype(BF16), v_ref[:, sl].astype(BF16), preferred_element_type=F32)
        o_ref[:, sl] = _pack_rows(o)


def mem_attention(proj, kv_mem, gq, gk, *, bsz, q_col_block, ts=512):
    t = proj.shape[0]
    seq = t // bsz
    mem_w = kv_mem.shape[1] // 2
    mem_tokens = kv_mem.shape[0] // bsz
    hd = mem_w // MEM_HEADS
    ts = min(ts, seq)
    nt = seq // ts
    return pl.pallas_call(
        functools.partial(_mem_attn_kernel, heads=MEM_HEADS, hd=hd),
        grid=(bsz, nt),
        in_specs=[pl.BlockSpec((ts, mem_w), lambda b, i: (b * nt + i, q_col_block)),
                  pl.BlockSpec((mem_tokens, mem_w), lambda b, i: (b, 0)),
                  pl.BlockSpec((mem_tokens, mem_w), lambda b, i: (b, 1)),
                  pl.BlockSpec((1, hd), lambda b, i: (0, 0)),
                  pl.BlockSpec((1, hd), lambda b, i: (0, 0))],
        out_specs=pl.BlockSpec((ts // 2, mem_w), lambda b, i: (b * nt + i, 0)),
        out_shape=jax.ShapeDtypeStruct((t // 2, mem_w), U32),
        compiler_params=_params("parallel", "parallel"),
        name="mem_attention",
    )(proj, kv_mem, kv_mem, gq.reshape(1, hd), gk.reshape(1, hd))


def _half_sign(shape, first, second):
    lane = lax.broadcasted_iota(jnp.int32, shape, len(shape) - 1)
    return jnp.where(lane < SSM_STATE, first, second).astype(F32)


def _swap_halves(x):
    return jnp.concatenate([x[:, SSM_STATE:], x[:, :SSM_STATE]], axis=1)


def _s5_prep_kernel(lr_ref, li_ref, ls_ref, b2_ref, c2_ref,
                    wc_ref, woc_ref, dd_ref, dec_ref, *, n_dec):
    L = S5_CHUNK
    lr = jnp.minimum(lr_ref[...], -1e-4)
    li = li_ref[...]
    step = jnp.exp(ls_ref[...])
    mag = jnp.exp(lr * step)
    a_re = mag * jnp.cos(li * step)
    a_im = mag * jnp.sin(li * step)
    p_re = a_re - 1.0
    den = lr * lr + li * li
    f_re = (p_re * lr + a_im * li) / den
    f_im = (a_im * lr - p_re * li) / den
    shape = lr.shape
    b2 = b2_ref[...]
    bb2 = f_re * b2 + f_im * (_swap_halves(b2) * _half_sign(shape, -1.0, 1.0))
    bb2s = _swap_halves(bb2) * _half_sign(shape, -1.0, 1.0)
    c2 = c2_ref[...]
    c_a = c2 * _half_sign(shape, 1.0, -1.0)
    c_b = -_swap_halves(c2)
    pw = [(jnp.ones(shape, F32), jnp.zeros(shape, F32))]
    for _ in range(L):
        pr, pi = pw[-1]
        pw.append((pr * a_re - pi * a_im, pr * a_im + pi * a_re))
    row_g = lax.broadcasted_iota(jnp.int32, shape, 0) // SSM_GROUP
    col_g = lax.broadcasted_iota(jnp.int32, shape, 1) // SSM_GROUP
    same_group = row_g == col_g
    for n in range(L):
        ar, ai = pw[L - 1 - n]
        wc_ref[0, n] = (ar * bb2 + ai * bb2s).astype(wc_ref.dtype)
        ar, ai = pw[n + 1]
        woc_ref[0, n] = (ar * c_a + ai * c_b).astype(woc_ref.dtype)
        ar, ai = pw[n]
        w0 = ar * c_a + ai * c_b
        kk = lax.dot_general(bb2, w0, (((1,), (1,)), ((), ())),
                             precision=lax.Precision.HIGHEST, preferred_element_type=F32)
        dd_ref[0, n] = jnp.where(same_group, kk, 0.0).astype(dd_ref.dtype)
    sgn = _half_sign(shape, -1.0, 1.0)
    pr, pi = pw[L]
    for k in range(n_dec):
        dec_ref[0, k] = pr
        dec_ref[0, n_dec + k] = pi * sgn
        pr, pi = pr * pr - pi * pi, 2.0 * pr * pi


def _shift_rows(x, sh):
    rows = lax.broadcasted_iota(jnp.int32, x.shape, 0)
    return jnp.where(rows >= sh, pltpu.roll(x, sh, 0), 0.0)


def _s5_main_kernel(u_ref, wc_ref, woc_ref, dd_ref, dec_ref, d_ref, y_ref,
                    win_s, wot_s, tt_s, *, bsz, seq, n_dec):
    L, H, GB = S5_CHUNK, SSM_GROUP, S5_BLOCK_GROUPS
    nc = seq // L

    @pl.when(pl.program_id(0) == 0)
    def _():
        win_s[...] = jnp.zeros(win_s.shape, win_s.dtype)
        wot_s[...] = jnp.zeros(wot_s.shape, wot_s.dtype)
        tt_s[...] = jnp.zeros(tt_s.shape, tt_s.dtype)

    for n in range(L):
        for g in range(GB):
            rows = slice(n * LANES + g * H, n * LANES + (g + 1) * H)
            cols = slice(g * LANES, (g + 1) * LANES)
            win_s[rows, cols] = wc_ref[0, n, g * H:(g + 1) * H, :]
            wot_s[rows, cols] = woc_ref[0, n, g * H:(g + 1) * H, :]
        for t in range(n, L):
            tt_s[n * LANES:(n + 1) * LANES, t * LANES:(t + 1) * LANES] = dd_ref[0, t - n]

    def piece(b, t):
        return u_ref[pl.ds(b * seq + t, nc, stride=L), :]

    ub = jnp.concatenate(
        [jnp.concatenate([piece(b, t).astype(BF16) for t in range(L)], axis=1)
         for b in range(bsz)], axis=0)
    z = jnp.dot(ub, win_s[...], preferred_element_type=F32)
    states = []
    for b in range(bsz):
        per_group = []
        for g in range(GB):
            x = z[b * nc:(b + 1) * nc, g * LANES:(g + 1) * LANES]
            for k in range(n_dec):
                if (1 << k) >= nc:
                    break
                xs = _shift_rows(x, 1 << k)
                a1 = dec_ref[0, k, g * H:g * H + 1, :]
                a2 = dec_ref[0, n_dec + k, g * H:g * H + 1, :]
                x = x + a1 * xs + a2 * pltpu.roll(xs, SSM_STATE, 1)
            per_group.append(_shift_rows(x, 1))
        states.append(jnp.concatenate(per_group, axis=1))
    s = jnp.concatenate(states, axis=0).astype(BF16)
    y = lax.dot_general(s, wot_s[...], (((1,), (1,)), ((), ())), preferred_element_type=F32)
    d_row = d_ref[...]
    tile_pairs = L * LANES // 256
    for j in range(tile_pairs):
        kk = (j + 1) * 256
        yj = y[:, j * 256:(j + 1) * 256] + jnp.dot(
            ub[:, :kk], tt_s[0:kk, j * 256:(j + 1) * 256], preferred_element_type=F32)
        for b in range(bsz):
            for tt in range(2):
                t = 2 * j + tt
                y_ref[pl.ds(b * seq + t, nc, stride=L), :] = (
                    yj[b * nc:(b + 1) * nc, tt * LANES:(tt + 1) * LANES] + d_row * piece(b, t))


def s5_scan(proj, lam_re, lam_im, log_step, b_re, b_im, c_re, c_im, d_skip, *, bsz):
    t = proj.shape[0]
    seq = t // bsz
    g = lam_re.shape[0]
    L, H, P, GB = S5_CHUNK, SSM_GROUP, SSM_STATE, S5_BLOCK_GROUPS
    nb = g // GB
    nc = seq // L
    n_dec = max(1, (nc - 1).bit_length())
    rows = lambda a: jnp.repeat(jnp.concatenate([a, a], axis=-1), H, axis=0)
    lr = rows(lam_re)
    li = rows(lam_im)
    ls = jnp.repeat(jnp.broadcast_to(log_step[:, None], (g, 2 * P)), H, axis=0)
    b2 = jnp.concatenate([jnp.swapaxes(b_re, 1, 2), jnp.swapaxes(b_im, 1, 2)],
                         axis=-1).reshape(g * H, 2 * P)
    c2 = jnp.concatenate([c_re, c_im], axis=-1).reshape(g * H, 2 * P)
    blk = pl.BlockSpec((LANES, 2 * P), lambda i: (i, 0))
    tab = pl.BlockSpec((1, L, LANES, LANES), lambda i: (i, 0, 0, 0))
    dec_spec = pl.BlockSpec((1, 2 * n_dec, LANES, LANES), lambda i: (i, 0, 0, 0))
    wc, woc, dd, dec = pl.pallas_call(
        functools.partial(_s5_prep_kernel, n_dec=n_dec),
        grid=(nb,),
        in_specs=[blk, blk, blk, blk, blk],
        out_specs=[tab, tab, tab, dec_spec],
        out_shape=[jax.ShapeDtypeStruct((nb, L, LANES, LANES), BF16),
                   jax.ShapeDtypeStruct((nb, L, LANES, LANES), BF16),
                   jax.ShapeDtypeStruct((nb, L, LANES, LANES), BF16),
                   jax.ShapeDtypeStruct((nb, 2 * n_dec, LANES, LANES), F32)],
        compiler_params=_params("parallel"),
        name="s5_prep",
    )(lr, li, ls, b2, c2)
    return pl.pallas_call(
        functools.partial(_s5_main_kernel, bsz=bsz, seq=seq, n_dec=n_dec),
        grid=(nb,),
        in_specs=[pl.BlockSpec((t, LANES), lambda i: (0, i)),
                  tab, tab, tab, dec_spec,
                  pl.BlockSpec((1, LANES), lambda i: (0, i))],
        out_specs=pl.BlockSpec((t, LANES), lambda i: (0, i)),
        out_shape=jax.ShapeDtypeStruct((t, g * H), F32),
        scratch_shapes=[pltpu.VMEM((L * LANES, GB * LANES), BF16),
                        pltpu.VMEM((L * LANES, GB * LANES), BF16),
                        pltpu.VMEM((L * LANES, L * LANES), BF16)],
        compiler_params=_params("arbitrary"),
        name="s5_main",
    )(proj, wc, woc, dd, dec, d_skip.reshape(1, g * H))


def _glu_kernel(y_ref, w_ref, o_ref):
    y = y_ref[...]
    c = math.sqrt(2.0 / math.pi)
    g = 0.5 * y * (1.0 + jnp.tanh(c * (y + 0.044715 * (y * y * y))))
    z = jnp.dot(g.astype(BF16), w_ref[...].astype(BF16), preferred_element_type=F32)
    o_ref[...] = _pack_rows(g * (1.0 / (1.0 + jnp.exp(-z))))


def gelu_glu(y, w_glu, layer, tm=512):
    m, n = y.shape
    tm = min(tm, m)
    return pl.pallas_call(
        _glu_kernel,
        grid=(m // tm,),
        in_specs=[pl.BlockSpec((tm, n), lambda i: (i, 0)),
                  pl.BlockSpec((None, n, n), lambda i: (layer, 0, 0))],
        out_specs=pl.BlockSpec((tm // 2, n), lambda i: (i, 0)),
        out_shape=jax.ShapeDtypeStruct((m // 2, n), U32),
        compiler_params=_params("parallel"),
        name="gelu_glu",
    )(y, w_glu)


def _sb_kernel(q_ref, k_ref, v_ref, o_ref, *, tq, hd, heads_per_step, scale):
    i = pl.program_id(2)
    row = lax.broadcasted_iota(jnp.int32, (tq, tq), 0)
    col = lax.broadcasted_iota(jnp.int32, (tq, tq), 1)
    later = jnp.where(row > col, 1.0, 0.0).astype(BF16)
    causal = col < row
    qs = [q_ref[:, h * hd:(h + 1) * hd].astype(BF16) for h in range(heads_per_step)]

    def block(h, j, carry, diagonal):
        acc, rem = carry
        start = pl.multiple_of(j * tq, tq)
        kj = k_ref[pl.ds(start, tq), h * hd:(h + 1) * hd]
        vj = v_ref[pl.ds(start, tq), h * hd:(h + 1) * hd]
        z = lax.dot_general(qs[h], kj, (((1,), (1,)), ((), ())),
                            preferred_element_type=F32) * scale
        sp = jnp.maximum(z, 0.0) + jnp.log(1.0 + jnp.exp(-jnp.abs(z)))
        spm = jnp.where(causal, sp, 0.0) if diagonal else sp
        hi = spm.astype(BF16)
        lo = (spm - hi.astype(F32)).astype(BF16)
        cs = (jnp.dot(hi, later, preferred_element_type=F32)
              + jnp.dot(lo, later, preferred_element_type=F32))
        w = jnp.exp(z - sp - cs + rem)
        if diagonal:
            w = jnp.where(causal, w, 0.0)
        acc = acc + jnp.dot(w.astype(BF16), vj, preferred_element_type=F32)
        rem = rem - jnp.sum(spm, axis=1, keepdims=True)
        return acc, rem

    init = (jnp.zeros((tq, hd), F32), jnp.zeros((tq, 1), F32))
    heads = range(heads_per_step)

    def diagonal_only():
        return [block(h, i, init, True) for h in heads]

    def diagonal_and_previous():
        return [block(h, i - 1, block(h, i, init, True), False) for h in heads]

    carries = lax.cond(i > 0, diagonal_and_previous, diagonal_only)

    for h in heads:
        def more(c):
            n, _, rem = c
            return jnp.logical_and(n < i, jnp.max(rem) > -F32_EXP_UNDERFLOW)

        def step(c, h=h):
            n, acc, rem = c
            acc, rem = block(h, i - 1 - n, (acc, rem), False)
            return n + 1, acc, rem

        _, acc, _ = lax.while_loop(more, step, (jnp.int32(1), *carries[h]))
        o_ref[:, h * hd:(h + 1) * hd] = _pack_rows(acc)


def stick_breaking(proj, kv, *, bsz, heads, tq=256, heads_per_step=2):
    t = proj.shape[0]
    seq = t // bsz
    hd = SB_HEAD_DIM
    tq = min(tq, seq)
    nq = seq // tq
    hg = heads // heads_per_step
    w = heads_per_step * hd
    return pl.pallas_call(
        functools.partial(_sb_kernel, tq=tq, hd=hd, heads_per_step=heads_per_step,
                          scale=1.0 / math.sqrt(hd)),
        grid=(bsz, hg, nq),
        in_specs=[pl.BlockSpec((tq, w), lambda b, h, i: (b * nq + i, h)),
                  pl.BlockSpec((seq, w), lambda b, h, i: (b, h)),
                  pl.BlockSpec((seq, w), lambda b, h, i: (b, hg + h))],
        out_specs=pl.BlockSpec((tq // 2, w), lambda b, h, i: (b * nq + i, h)),
        out_shape=jax.ShapeDtypeStruct((t // 2, heads * hd), U32),
        compiler_params=_params("parallel", "parallel", "arbitrary"),
        name="stick_breaking",
    )(proj, kv, kv)


def _ffn_up_kernel(x_ref, wa_ref, wb_ref, cwa_ref, cwb_ref, cba_ref, cbb_ref, o_ref,
                   bufa_ref, bufb_ref, haloa_ref, halob_ref, *, tm, sub, tiles_per_seq):
    i = pl.program_id(0)
    j = pl.program_id(1)

    @pl.when(i % tiles_per_seq == 0)
    def _():
        bufa_ref[0:SUBLANES, :] = jnp.zeros((SUBLANES, bufa_ref.shape[1]), F32)
        bufb_ref[0:SUBLANES, :] = jnp.zeros((SUBLANES, bufb_ref.shape[1]), F32)

    @pl.when(i % tiles_per_seq != 0)
    def _():
        bufa_ref[0:SUBLANES, :] = haloa_ref[j]
        bufb_ref[0:SUBLANES, :] = halob_ref[j]

    wa = wa_ref[...].astype(BF16)
    wb = wb_ref[...].astype(BF16)

    def conv(x, r0, buf_ref, w, cw_ref, cb_ref):
        up = jnp.dot(x, w, preferred_element_type=F32)
        base = SUBLANES + r0
        buf_ref[base:base + sub, :] = up
        return (buf_ref[base - 2:base - 2 + sub, :] * cw_ref[0:1, :]
                + buf_ref[base - 1:base - 1 + sub, :] * cw_ref[1:2, :]
                + up * cw_ref[2:3, :] + cb_ref[...])

    for r0 in range(0, tm, sub):
        x = _unpack_rows(x_ref[r0 // 2:(r0 + sub) // 2, :])
        a = conv(x, r0, bufa_ref, wa, cwa_ref, cba_ref)
        b = conv(x, r0, bufb_ref, wb, cwb_ref, cbb_ref)
        o_ref[r0 // 2:(r0 + sub) // 2, :] = _pack_rows(a * (1.0 / (1.0 + jnp.exp(-a))) * b)
    haloa_ref[j] = bufa_ref[tm:tm + SUBLANES, :]
    halob_ref[j] = bufb_ref[tm:tm + SUBLANES, :]


def ffn_up(xn, w_up, conv_w, conv_b, layer, *, seq, tm=2048, tf=256, sub=256):
    t = 2 * xn.shape[0]
    d = xn.shape[1]
    ff = w_up.shape[2] // 2
    tm = min(tm, seq)
    sub = min(sub, tm)
    nf = ff // tf
    cb = conv_b.reshape(conv_b.shape[0], 1, 2 * ff)
    return pl.pallas_call(
        functools.partial(_ffn_up_kernel, tm=tm, sub=sub, tiles_per_seq=seq // tm),
        grid=(t // tm, nf),
        in_specs=[pl.BlockSpec((tm // 2, d), lambda i, j: (i, 0), pipeline_mode=pl.Buffered(1)),
                  pl.BlockSpec((None, d, tf), lambda i, j: (layer, 0, j)),
                  pl.BlockSpec((None, d, tf), lambda i, j: (layer, 0, nf + j)),
                  pl.BlockSpec((None, CONV_W, tf), lambda i, j: (layer, 0, j)),
                  pl.BlockSpec((None, CONV_W, tf), lambda i, j: (layer, 0, nf + j)),
                  pl.BlockSpec((None, 1, tf), lambda i, j: (layer, 0, j)),
                  pl.BlockSpec((None, 1, tf), lambda i, j: (layer, 0, nf + j))],
        out_specs=pl.BlockSpec((tm // 2, tf), lambda i, j: (i, j)),
        out_shape=jax.ShapeDtypeStruct((t // 2, ff), U32),
        scratch_shapes=[pltpu.VMEM((tm + SUBLANES, tf), F32),
                        pltpu.VMEM((tm + SUBLANES, tf), F32),
                        pltpu.VMEM((nf, SUBLANES, tf), F32),
                        pltpu.VMEM((nf, SUBLANES, tf), F32)],
        compiler_params=_params("arbitrary", "arbitrary"),
        name="ffn_up",
    )(xn, w_up, w_up, conv_w, conv_w, cb, cb)


def kernel(x, mem, norm_mix_g, w_in, w_out, mem_norm_g, w_mem_kv, mem_q_norm_g, mem_k_norm_g,
           s5_lam_re, s5_lam_im, s5_log_step, s5_b_re, s5_b_im, s5_c_re, s5_c_im, s5_d,
           s5_w_glu, kv_norm_g, w_kv_shared, norm_ffn_g, w_ffn_up, ffn_conv_w, ffn_conv_b,
           w_ffn_down):
    bsz, seq, d = x.shape
    depth = w_in.shape[0]
    n_a = s5_lam_re.shape[0]
    mem2 = mem.reshape(bsz * mem.shape[1], d)
    ssm_w = s5_w_glu.shape[1]
    mem_w = w_mem_kv.shape[2] // 2
    w_down = w_ffn_down.astype(BF16)
    h = x.reshape(bsz * seq, d)
    kv_sh = None
    for i in range(depth):
        proj = matmul([rmsnorm(h, norm_mix_g[i])], w_in, i, name="w_in")
        kv_mem = matmul([rmsnorm(mem2, mem_norm_g[i])], w_mem_kv, i, name="w_mem_kv")
        m_out = mem_attention(proj, kv_mem, mem_q_norm_g[i], mem_k_norm_g[i],
                              bsz=bsz, q_col_block=ssm_w // mem_w)
        if i < n_a:
            y = s5_scan(proj, s5_lam_re[i], s5_lam_im[i], s5_log_step[i], s5_b_re[i],
                        s5_b_im[i], s5_c_re[i], s5_c_im[i], s5_d[i], bsz=bsz)
            p_out = gelu_glu(y, s5_w_glu, i)
        else:
            p_out = stick_breaking(proj, kv_sh, bsz=bsz, heads=ssm_w // SB_HEAD_DIM)
        h = matmul([p_out, m_out], w_out, i, res=h, name="w_out")
        act = ffn_up(rmsnorm(h, norm_ffn_g[i]), w_ffn_up, ffn_conv_w, ffn_conv_b, i, seq=seq)
        h = matmul([act], w_down, i, res=h, tm=512, tn=512, name="ffn_down")
        if i == n_a - 1:
            kv_sh = matmul([rmsnorm(h, kv_norm_g)], w_kv_shared[None], 0, out_dtype=BF16,
                           name="w_kv_shared")
    return h.reshape(bsz, seq, d)
```

```python
import functools
import math

import jax
import jax.numpy as jnp
from jax import lax
from jax.experimental import pallas as pl
from jax.experimental.pallas import tpu as pltpu

F32 = jnp.float32
BF16 = jnp.bfloat16
U32 = jnp.uint32
EPS = 1e-6

V7X_VMEM_BYTES = 64 * 1024 * 1024
VMEM_LIMIT_BYTES = V7X_VMEM_BYTES - 8 * 1024 * 1024
SUBLANES = 8
LANES = 128

SSM_GROUP = 16
SSM_STATE = 64
S5_CHUNK = 16
S5_BLOCK_GROUPS = LANES // SSM_GROUP
SB_HEAD_DIM = 128
MEM_HEADS = 4
CONV_W = 3
F32_EXP_UNDERFLOW = 104.0


def _params(*semantics):
    return pltpu.CompilerParams(dimension_semantics=semantics,
                                vmem_limit_bytes=VMEM_LIMIT_BYTES)


def _pack_rows(y):
    return pltpu.bitcast(y.astype(BF16), U32)


def _unpack_rows(p):
    return pltpu.bitcast(p, BF16)


def _rms_kernel(x_ref, g_ref, o_ref):
    x = x_ref[...]
    ms = jnp.mean(x * x, axis=-1, keepdims=True)
    o_ref[...] = _pack_rows(x * lax.rsqrt(ms + EPS) * g_ref[...])


def rmsnorm(x, g, tm=512):
    m, d = x.shape
    tm = min(tm, m)
    return pl.pallas_call(
        _rms_kernel,
        grid=(m // tm,),
        in_specs=[pl.BlockSpec((tm, d), lambda i: (i, 0)),
                  pl.BlockSpec((1, d), lambda i: (0, 0))],
        out_specs=pl.BlockSpec((tm // 2, d), lambda i: (i, 0)),
        out_shape=jax.ShapeDtypeStruct((m // 2, d), U32),
        compiler_params=_params("parallel"),
        name="rmsnorm",
    )(x, g.reshape(1, d))


def _mm_kernel(*refs, n_lhs, has_res):
    a_refs = refs[:n_lhs]
    w_refs = refs[n_lhs:2 * n_lhs]
    r_ref = refs[2 * n_lhs] if has_res else None
    o_ref = refs[-1]
    acc = None
    for a_ref, w_ref in zip(a_refs, w_refs):
        d = jnp.dot(_unpack_rows(a_ref[...]), w_ref[...].astype(BF16),
                    preferred_element_type=F32)
        acc = d if acc is None else acc + d
    if has_res:
        acc = r_ref[...] + acc
    o_ref[...] = acc.astype(o_ref.dtype)


def matmul(lhs, w, layer, res=None, out_dtype=F32, tm=1024, tn=512, name="matmul"):
    m = 2 * lhs[0].shape[0]
    n = w.shape[2]
    tm = min(tm, m)
    tn = min(tn, n)
    in_specs, args = [], []
    for a in lhs:
        in_specs.append(pl.BlockSpec((tm // 2, a.shape[1]), lambda j, i: (i, 0)))
        args.append(a)
    row = 0
    for a in lhs:
        ka = a.shape[1]
        assert row % ka == 0
        in_specs.append(pl.BlockSpec((None, ka, tn), lambda j, i, b=row // ka: (layer, b, j)))
        args.append(w)
        row += ka
    assert row == w.shape[1]
    if res is not None:
        in_specs.append(pl.BlockSpec((tm, tn), lambda j, i: (i, j)))
        args.append(res)
    return pl.pallas_call(
        functools.partial(_mm_kernel, n_lhs=len(lhs), has_res=res is not None),
        grid=(n // tn, m // tm),
        in_specs=in_specs,
        out_specs=pl.BlockSpec((tm, tn), lambda j, i: (i, j)),
        out_shape=jax.ShapeDtypeStruct((m, n), out_dtype),
        compiler_params=_params("parallel", "parallel"),
        name=name,
    )(*args)


def _mem_attn_kernel(q_ref, k_ref, v_ref, gq_ref, gk_ref, o_ref, *, heads, hd):
    gq = gq_ref[...]
    gk = gk_ref[...]
    inv_sqrt = 1.0 / math.sqrt(hd)
    for h in range(heads):
        sl = slice(h * hd, (h + 1) * hd)
        q = q_ref[:, sl]
        k = k_ref[:, sl]
        qn = (q * lax.rsqrt(jnp.mean(q * q, axis=-1, keepdims=True) + EPS) * gq).astype(BF16)
        kn = (k * lax.rsqrt(jnp.mean(k * k, axis=-1, keepdims=True) + EPS) * gk).astype(BF16)
        logits = lax.dot_general(qn, kn, (((1,), (1,)), ((), ())),
                                 preferred_element_type=F32) * inv_sqrt
        mx = jnp.max(logits, axis=-1, keepdims=True)
        p = jnp.exp(logits - mx)
        p = p / jnp.sum(p, axis=-1, keepdims=True)
        o = jnp.dot(p.astype(BF16), v_ref[:, sl].astype(BF16), preferred_element_type=F32)
        o_ref[:, sl] = _pack_rows(o)


def mem_attention(proj, kv_mem, gq, gk, *, bsz, q_col_block, ts=512):
    t = proj.shape[0]
    seq = t // bsz
    mem_w = kv_mem.shape[1] // 2
    mem_tokens = kv_mem.shape[0] // bsz
    hd = mem_w // MEM_HEADS
    ts = min(ts, seq)
    nt = seq // ts
    return pl.pallas_call(
        functools.partial(_mem_attn_kernel, heads=MEM_HEADS, hd=hd),
        grid=(bsz, nt),
        in_specs=[pl.BlockSpec((ts, mem_w), lambda b, i: (b * nt + i, q_col_block)),
                  pl.BlockSpec((mem_tokens, mem_w), lambda b, i: (b, 0)),
                  pl.BlockSpec((mem_tokens, mem_w), lambda b, i: (b, 1)),
                  pl.BlockSpec((1, hd), lambda b, i: (0, 0)),
                  pl.BlockSpec((1, hd), lambda b, i: (0, 0))],
        out_specs=pl.BlockSpec((ts // 2, mem_w), lambda b, i: (b * nt + i, 0)),
        out_shape=jax.ShapeDtypeStruct((t // 2, mem_w), U32),
        compiler_params=_params("parallel", "parallel"),
        name="mem_attention",
    )(proj, kv_mem, kv_mem, gq.reshape(1, hd), gk.reshape(1, hd))


def _half_sign(shape, first, second):
    lane = lax.broadcasted_iota(jnp.int32, shape, len(shape) - 1)
    return jnp.where(lane < SSM_STATE, first, second).astype(F32)


def _swap_halves(x):
    return jnp.concatenate([x[:, SSM_STATE:], x[:, :SSM_STATE]], axis=1)


def _s5_prep_kernel(lr_ref, li_ref, ls_ref, b2_ref, c2_ref,
                    wc_ref, woc_ref, dd_ref, dec_ref, *, n_dec):
    L = S5_CHUNK
    lr = jnp.minimum(lr_ref[...], -1e-4)
    li = li_ref[...]
    step = jnp.exp(ls_ref[...])
    mag = jnp.exp(lr * step)
    a_re = mag * jnp.cos(li * step)
    a_im = mag * jnp.sin(li * step)
    p_re = a_re - 1.0
    den = lr * lr + li * li
    f_re = (p_re * lr + a_im * li) / den
    f_im = (a_im * lr - p_re * li) / den
    shape = lr.shape
    b2 = b2_ref[...]
    bb2 = f_re * b2 + f_im * (_swap_halves(b2) * _half_sign(shape, -1.0, 1.0))
    bb2s = _swap_halves(bb2) * _half_sign(shape, -1.0, 1.0)
    c2 = c2_ref[...]
    c_a = c2 * _half_sign(shape, 1.0, -1.0)
    c_b = -_swap_halves(c2)
    pw = [(jnp.ones(shape, F32), jnp.zeros(shape, F32))]
    for _ in range(L):
        pr, pi = pw[-1]
        pw.append((pr * a_re - pi * a_im, pr * a_im + pi * a_re))
    row_g = lax.broadcasted_iota(jnp.int32, shape, 0) // SSM_GROUP
    col_g = lax.broadcasted_iota(jnp.int32, shape, 1) // SSM_GROUP
    same_group = row_g == col_g
    for n in range(L):
        ar, ai = pw[L - 1 - n]
        wc_ref[0, n] = (ar * bb2 + ai * bb2s).astype(wc_ref.dtype)
        ar, ai = pw[n + 1]
        woc_ref[0, n] = (ar * c_a + ai * c_b).astype(woc_ref.dtype)
        ar, ai = pw[n]
        w0 = ar * c_a + ai * c_b
        kk = lax.dot_general(bb2, w0, (((1,), (1,)), ((), ())),
                             precision=lax.Precision.HIGHEST, preferred_element_type=F32)
        dd_ref[0, n] = jnp.where(same_group, kk, 0.0).astype(dd_ref.dtype)
    sgn = _half_sign(shape, -1.0, 1.0)
    pr, pi = pw[L]
    for k in range(n_dec):
        dec_ref[0, k] = pr
        dec_ref[0, n_dec + k] = pi * sgn
        pr, pi = pr * pr - pi * pi, 2.0 * pr * pi


def _shift_rows(x, sh):
    rows = lax.broadcasted_iota(jnp.int32, x.shape, 0)
    return jnp.where(rows >= sh, pltpu.roll(x, sh, 0), 0.0)


def _s5_main_kernel(u_ref, wc_ref, woc_ref, dd_ref, dec_ref, d_ref, y_ref,
                    win_s, wot_s, tt_s, *, bsz, seq, n_dec):
    L, H, GB = S5_CHUNK, SSM_GROUP, S5_BLOCK_GROUPS
    nc = seq // L

    @pl.when(pl.program_id(0) == 0)
    def _():
        win_s[...] = jnp.zeros(win_s.shape, win_s.dtype)
        wot_s[...] = jnp.zeros(wot_s.shape, wot_s.dtype)
        tt_s[...] = jnp.zeros(tt_s.shape, tt_s.dtype)

    for n in range(L):
        for g in range(GB):
            rows = slice(n * LANES + g * H, n * LANES + (g + 1) * H)
            cols = slice(g * LANES, (g + 1) * LANES)
            win_s[rows, cols] = wc_ref[0, n, g * H:(g + 1) * H, :]
            wot_s[rows, cols] = woc_ref[0, n, g * H:(g + 1) * H, :]
        for t in range(n, L):
            tt_s[n * LANES:(n + 1) * LANES, t * LANES:(t + 1) * LANES] = dd_ref[0, t - n]

    def piece(b, t):
        return u_ref[pl.ds(b * seq + t, nc, stride=L), :]

    ub = jnp.concatenate(
        [jnp.concatenate([piece(b, t).astype(BF16) for t in range(L)], axis=1)
         for b in range(bsz)], axis=0)
    z = jnp.dot(ub, win_s[...], preferred_element_type=F32)
    states = []
    for b in range(bsz):
        per_group = []
        for g in range(GB):
            x = z[b * nc:(b + 1) * nc, g * LANES:(g + 1) * LANES]
            for k in range(n_dec):
                if (1 << k) >= nc:
                    break
                xs = _shift_rows(x, 1 << k)
                a1 = dec_ref[0, k, g * H:g * H + 1, :]
                a2 = dec_ref[0, n_dec + k, g * H:g * H + 1, :]
                x = x + a1 * xs + a2 * pltpu.roll(xs, SSM_STATE, 1)
            per_group.append(_shift_rows(x, 1))
        states.append(jnp.concatenate(per_group, axis=1))
    s = jnp.concatenate(states, axis=0).astype(BF16)
    y = lax.dot_general(s, wot_s[...], (((1,), (1,)), ((), ())), preferred_element_type=F32)
    d_row = d_ref[...]
    tile_pairs = L * LANES // 256
    for j in range(tile_pairs):
        kk = (j + 1) * 256
        yj = y[:, j * 256:(j + 1) * 256] + jnp.dot(
            ub[:, :kk], tt_s[0:kk, j * 256:(j + 1) * 256], preferred_element_type=F32)
        for b in range(bsz):
            for tt in range(2):
                t = 2 * j + tt
                y_ref[pl.ds(b * seq + t, nc, stride=L), :] = (
                    yj[b * nc:(b + 1) * nc, tt * LANES:(tt + 1) * LANES] + d_row * piece(b, t))


def s5_scan(proj, lam_re, lam_im, log_step, b_re, b_im, c_re, c_im, d_skip, *, bsz):
    t = proj.shape[0]
    seq = t // bsz
    g = lam_re.shape[0]
    L, H, P, GB = S5_CHUNK, SSM_GROUP, SSM_STATE, S5_BLOCK_GROUPS
    nb = g // GB
    nc = seq // L
    n_dec = max(1, (nc - 1).bit_length())
    rows = lambda a: jnp.repeat(jnp.concatenate([a, a], axis=-1), H, axis=0)
    lr = rows(lam_re)
    li = rows(lam_im)
    ls = jnp.repeat(jnp.broadcast_to(log_step[:, None], (g, 2 * P)), H, axis=0)
    b2 = jnp.concatenate([jnp.swapaxes(b_re, 1, 2), jnp.swapaxes(b_im, 1, 2)],
                         axis=-1).reshape(g * H, 2 * P)
    c2 = jnp.concatenate([c_re, c_im], axis=-1).reshape(g * H, 2 * P)
    blk = pl.BlockSpec((LANES, 2 * P), lambda i: (i, 0))
    tab = pl.BlockSpec((1, L, LANES, LANES), lambda i: (i, 0, 0, 0))
    dec_spec = pl.BlockSpec((1, 2 * n_dec, LANES, LANES), lambda i: (i, 0, 0, 0))
    wc, woc, dd, dec = pl.pallas_call(
        functools.partial(_s5_prep_kernel, n_dec=n_dec),
        grid=(nb,),
        in_specs=[blk, blk, blk, blk, blk],
        out_specs=[tab, tab, tab, dec_spec],
        out_shape=[jax.ShapeDtypeStruct((nb, L, LANES, LANES), BF16),
                   jax.ShapeDtypeStruct((nb, L, LANES, LANES), BF16),
                   jax.ShapeDtypeStruct((nb, L, LANES, LANES), BF16),
                   jax.ShapeDtypeStruct((nb, 2 * n_dec, LANES, LANES), F32)],
        compiler_params=_params("parallel"),
        name="s5_prep",
    )(lr, li, ls, b2, c2)
    return pl.pallas_call(
        functools.partial(_s5_main_kernel, bsz=bsz, seq=seq, n_dec=n_dec),
        grid=(nb,),
        in_specs=[pl.BlockSpec((t, LANES), lambda i: (0, i)),
                  tab, tab, tab, dec_spec,
                  pl.BlockSpec((1, LANES), lambda i: (0, i))],
        out_specs=pl.BlockSpec((t, LANES), lambda i: (0, i)),
        out_shape=jax.ShapeDtypeStruct((t, g * H), F32),
        scratch_shapes=[pltpu.VMEM((L * LANES, GB * LANES), BF16),
                        pltpu.VMEM((L * LANES, GB * LANES), BF16),
                        pltpu.VMEM((L * LANES, L * LANES), BF16)],
        compiler_params=_params("arbitrary"),
        name="s5_main",
    )(proj, wc, woc, dd, dec, d_skip.reshape(1, g * H))


def _glu_kernel(y_ref, w_ref, o_ref):
    y = y_ref[...]
    c = math.sqrt(2.0 / math.pi)
    g = 0.5 * y * (1.0 + jnp.tanh(c * (y + 0.044715 * (y * y * y))))
    z = jnp.dot(g.astype(BF16), w_ref[...].astype(BF16), preferred_element_type=F32)
    o_ref[...] = _pack_rows(g * (1.0 / (1.0 + jnp.exp(-z))))


def gelu_glu(y, w_glu, layer, tm=512):
    m, n = y.shape
    tm = min(tm, m)
    return pl.pallas_call(
        _glu_kernel,
        grid=(m // tm,),
        in_specs=[pl.BlockSpec((tm, n), lambda i: (i, 0)),
                  pl.BlockSpec((None, n, n), lambda i: (layer, 0, 0))],
        out_specs=pl.BlockSpec((tm // 2, n), lambda i: (i, 0)),
        out_shape=jax.ShapeDtypeStruct((m // 2, n), U32),
        compiler_params=_params("parallel"),
        name="gelu_glu",
    )(y, w_glu)


def _sb_kernel(q_ref, k_ref, v_ref, o_ref, *, tq, hd, heads_per_step, scale):
    i = pl.program_id(2)
    row = lax.broadcasted_iota(jnp.int32, (tq, tq), 0)
    col = lax.broadcasted_iota(jnp.int32, (tq, tq), 1)
    later = jnp.where(row > col, 1.0, 0.0).astype(BF16)
    causal = col < row
    qs = [q_ref[:, h * hd:(h + 1) * hd].astype(BF16) for h in range(heads_per_step)]

    def block(h, j, carry, diagonal):
        acc, rem = carry
        start = pl.multiple_of(j * tq, tq)
        kj = k_ref[pl.ds(start, tq), h * hd:(h + 1) * hd]
        vj = v_ref[pl.ds(start, tq), h * hd:(h + 1) * hd]
        z = lax.dot_general(qs[h], kj, (((1,), (1,)), ((), ())),
                            preferred_element_type=F32) * scale
        sp = jnp.maximum(z, 0.0) + jnp.log(1.0 + jnp.exp(-jnp.abs(z)))
        spm = jnp.where(causal, sp, 0.0) if diagonal else sp
        hi = spm.astype(BF16)
        lo = (spm - hi.astype(F32)).astype(BF16)
        cs = (jnp.dot(hi, later, preferred_element_type=F32)
              + jnp.dot(lo, later, preferred_element_type=F32))
        w = jnp.exp(z - sp - cs + rem)
        if diagonal:
            w = jnp.where(causal, w, 0.0)
        acc = acc + jnp.dot(w.astype(BF16), vj, preferred_element_type=F32)
        rem = rem - jnp.sum(spm, axis=1, keepdims=True)
        return acc, rem

    init = (jnp.zeros((tq, hd), F32), jnp.zeros((tq, 1), F32))
    heads = range(heads_per_step)

    def diagonal_only():
        return [block(h, i, init, True) for h in heads]

    def diagonal_and_previous():
        return [block(h, i - 1, block(h, i, init, True), False) for h in heads]

    carries = lax.cond(i > 0, diagonal_and_previous, diagonal_only)

    for h in heads:
        def more(c):
            n, _, rem = c
            return jnp.logical_and(n < i, jnp.max(rem) > -F32_EXP_UNDERFLOW)

        def step(c, h=h):
            n, acc, rem = c
            acc, rem = block(h, i - 1 - n, (acc, rem), False)
            return n + 1, acc, rem

        _, acc, _ = lax.while_loop(more, step, (jnp.int32(1), *carries[h]))
        o_ref[:, h * hd:(h + 1) * hd] = _pack_rows(acc)


def stick_breaking(proj, kv, *, bsz, heads, tq=256, heads_per_step=4):
    t = proj.shape[0]
    seq = t // bsz
    hd = SB_HEAD_DIM
    tq = min(tq, seq)
    nq = seq // tq
    hg = heads // heads_per_step
    w = heads_per_step * hd
    return pl.pallas_call(
        functools.partial(_sb_kernel, tq=tq, hd=hd, heads_per_step=heads_per_step,
                          scale=1.0 / math.sqrt(hd)),
        grid=(bsz, hg, nq),
        in_specs=[pl.BlockSpec((tq, w), lambda b, h, i: (b * nq + i, h)),
                  pl.BlockSpec((seq, w), lambda b, h, i: (b, h)),
                  pl.BlockSpec((seq, w), lambda b, h, i: (b, hg + h))],
        out_specs=pl.BlockSpec((tq // 2, w), lambda b, h, i: (b * nq + i, h)),
        out_shape=jax.ShapeDtypeStruct((t // 2, heads * hd), U32),
        compiler_params=_params("parallel", "parallel", "arbitrary"),
        name="stick_breaking",
    )(proj, kv, kv)


def _causal_conv(up, tail, cw_ref, cb_ref):
    r8 = lax.broadcasted_iota(jnp.int32, tail.shape, 0)
    acc = up * cw_ref[CONV_W - 1:CONV_W, :] + cb_ref[...]
    for sh in range(1, CONV_W):
        rolled = pltpu.roll(up, sh, 0)
        top = jnp.where(r8 < sh, pltpu.roll(tail, sh, 0), rolled[0:SUBLANES])
        shifted = jnp.concatenate([top, rolled[SUBLANES:]], axis=0)
        acc = acc + shifted * cw_ref[CONV_W - 1 - sh:CONV_W - sh, :]
    return acc


def _ffn_up_kernel(x_ref, wa_ref, wb_ref, cwa_ref, cwb_ref, cba_ref, cbb_ref, o_ref,
                   haloa_ref, halob_ref, *, tm, sub, tiles_per_seq):
    i = pl.program_id(0)
    j = pl.program_id(1)

    @pl.when(i % tiles_per_seq == 0)
    def _():
        haloa_ref[j] = jnp.zeros(haloa_ref.shape[1:], F32)
        halob_ref[j] = jnp.zeros(halob_ref.shape[1:], F32)

    tail_a = haloa_ref[j]
    tail_b = halob_ref[j]
    wa = wa_ref[...].astype(BF16)
    wb = wb_ref[...].astype(BF16)
    for r0 in range(0, tm, sub):
        x = _unpack_rows(x_ref[r0 // 2:(r0 + sub) // 2, :])
        up_a = jnp.dot(x, wa, preferred_element_type=F32)
        up_b = jnp.dot(x, wb, preferred_element_type=F32)
        a = _causal_conv(up_a, tail_a, cwa_ref, cba_ref)
        b = _causal_conv(up_b, tail_b, cwb_ref, cbb_ref)
        tail_a = up_a[sub - SUBLANES:]
        tail_b = up_b[sub - SUBLANES:]
        o_ref[r0 // 2:(r0 + sub) // 2, :] = _pack_rows(a * (1.0 / (1.0 + jnp.exp(-a))) * b)
    haloa_ref[j] = tail_a
    halob_ref[j] = tail_b


def ffn_up(xn, w_up, conv_w, conv_b, layer, *, seq, tm=2048, tf=256, sub=128):
    t = 2 * xn.shape[0]
    d = xn.shape[1]
    ff = w_up.shape[2] // 2
    tm = min(tm, seq)
    sub = min(sub, tm)
    nf = ff // tf
    cb = conv_b.reshape(conv_b.shape[0], 1, 2 * ff)
    return pl.pallas_call(
        functools.partial(_ffn_up_kernel, tm=tm, sub=sub, tiles_per_seq=seq // tm),
        grid=(t // tm, nf),
        in_specs=[pl.BlockSpec((tm // 2, d), lambda i, j: (i, 0), pipeline_mode=pl.Buffered(1)),
                  pl.BlockSpec((None, d, tf), lambda i, j: (layer, 0, j)),
                  pl.BlockSpec((None, d, tf), lambda i, j: (layer, 0, nf + j)),
                  pl.BlockSpec((None, CONV_W, tf), lambda i, j: (layer, 0, j)),
                  pl.BlockSpec((None, CONV_W, tf), lambda i, j: (layer, 0, nf + j)),
                  pl.BlockSpec((None, 1, tf), lambda i, j: (layer, 0, j)),
                  pl.BlockSpec((None, 1, tf), lambda i, j: (layer, 0, nf + j))],
        out_specs=pl.BlockSpec((tm // 2, tf), lambda i, j: (i, j)),
        out_shape=jax.ShapeDtypeStruct((t // 2, ff), U32),
        scratch_shapes=[pltpu.VMEM((nf, SUBLANES, tf), F32),
                        pltpu.VMEM((nf, SUBLANES, tf), F32)],
        compiler_params=_params("arbitrary", "arbitrary"),
        name="ffn_up",
    )(xn, w_up, w_up, conv_w, conv_w, cb, cb)


def kernel(x, mem, norm_mix_g, w_in, w_out, mem_norm_g, w_mem_kv, mem_q_norm_g, mem_k_norm_g,
           s5_lam_re, s5_lam_im, s5_log_step, s5_b_re, s5_b_im, s5_c_re, s5_c_im, s5_d,
           s5_w_glu, kv_norm_g, w_kv_shared, norm_ffn_g, w_ffn_up, ffn_conv_w, ffn_conv_b,
           w_ffn_down):
    bsz, seq, d = x.shape
    depth = w_in.shape[0]
    n_a = s5_lam_re.shape[0]
    mem2 = mem.reshape(bsz * mem.shape[1], d)
    ssm_w = s5_w_glu.shape[1]
    mem_w = w_mem_kv.shape[2] // 2
    w_down = w_ffn_down.astype(BF16)
    h = x.reshape(bsz * seq, d)
    kv_sh = None
    for i in range(depth):
        proj = matmul([rmsnorm(h, norm_mix_g[i])], w_in, i, name="w_in")
        kv_mem = matmul([rmsnorm(mem2, mem_norm_g[i])], w_mem_kv, i, name="w_mem_kv")
        m_out = mem_attention(proj, kv_mem, mem_q_norm_g[i], mem_k_norm_g[i],
                              bsz=bsz, q_col_block=ssm_w // mem_w)
        if i < n_a:
            y = s5_scan(proj, s5_lam_re[i], s5_lam_im[i], s5_log_step[i], s5_b_re[i],
                        s5_b_im[i], s5_c_re[i], s5_c_im[i], s5_d[i], bsz=bsz)
            p_out = gelu_glu(y, s5_w_glu, i)
        else:
            p_out = stick_breaking(proj, kv_sh, bsz=bsz, heads=ssm_w // SB_HEAD_DIM)
        h = matmul([p_out, m_out], w_out, i, res=h, tm=512, tn=1024, name="w_out")
        act = ffn_up(rmsnorm(h, norm_ffn_g[i]), w_ffn_up, ffn_conv_w, ffn_conv_b, i, seq=seq)
        h = matmul([act], w_down, i, res=h, tm=512, tn=512, name="ffn_down")
        if i == n_a - 1:
            kv_sh = matmul([rmsnorm(h, kv_norm_g)], w_kv_shared[None], 0, out_dtype=BF16,
                           name="w_kv_shared")
    return h.reshape(bsz, seq, d)
```

```python
import functools
import math

import jax
import jax.numpy as jnp
from jax import lax
from jax.experimental import pallas as pl
from jax.experimental.pallas import tpu as pltpu

F32 = jnp.float32
BF16 = jnp.bfloat16
U32 = jnp.uint32
EPS = 1e-6

V7X_VMEM_BYTES = 64 * 1024 * 1024
VMEM_LIMIT_BYTES = V7X_VMEM_BYTES - 8 * 1024 * 1024
SUBLANES = 8
LANES = 128

SSM_GROUP = 16
SSM_STATE = 64
S5_CHUNK = 16
S5_BLOCK_GROUPS = LANES // SSM_GROUP
SB_HEAD_DIM = 128
MEM_HEADS = 4
CONV_W = 3
F32_EXP_UNDERFLOW = 104.0


def _params(*semantics):
    return pltpu.CompilerParams(dimension_semantics=semantics,
                                vmem_limit_bytes=VMEM_LIMIT_BYTES)


def _pack_rows(y):
    return pltpu.bitcast(y.astype(BF16), U32)


def _unpack_rows(p):
    return pltpu.bitcast(p, BF16)


def _rms_kernel(x_ref, g_ref, o_ref):
    x = x_ref[...]
    ms = jnp.mean(x * x, axis=-1, keepdims=True)
    o_ref[...] = _pack_rows(x * lax.rsqrt(ms + EPS) * g_ref[...])


def rmsnorm(x, g, tm=512):
    m, d = x.shape
    tm = min(tm, m)
    return pl.pallas_call(
        _rms_kernel,
        grid=(m // tm,),
        in_specs=[pl.BlockSpec((tm, d), lambda i: (i, 0)),
                  pl.BlockSpec((1, d), lambda i: (0, 0))],
        out_specs=pl.BlockSpec((tm // 2, d), lambda i: (i, 0)),
        out_shape=jax.ShapeDtypeStruct((m // 2, d), U32),
        compiler_params=_params("parallel"),
        name="rmsnorm",
    )(x, g.reshape(1, d))


def _mm_kernel(*refs, n_lhs, has_res):
    a_refs = refs[:n_lhs]
    w_refs = refs[n_lhs:2 * n_lhs]
    r_ref = refs[2 * n_lhs] if has_res else None
    o_ref = refs[-1]
    acc = None
    for a_ref, w_ref in zip(a_refs, w_refs):
        d = jnp.dot(_unpack_rows(a_ref[...]), w_ref[...].astype(BF16),
                    preferred_element_type=F32)
        acc = d if acc is None else acc + d
    if has_res:
        acc = r_ref[...] + acc
    o_ref[...] = acc.astype(o_ref.dtype)


def matmul(lhs, w, layer, res=None, out_dtype=F32, tm=1024, tn=512, name="matmul"):
    m = 2 * lhs[0].shape[0]
    n = w.shape[2]
    tm = min(tm, m)
    tn = min(tn, n)
    in_specs, args = [], []
    for a in lhs:
        in_specs.append(pl.BlockSpec((tm // 2, a.shape[1]), lambda j, i: (i, 0)))
        args.append(a)
    row = 0
    for a in lhs:
        ka = a.shape[1]
        assert row % ka == 0
        in_specs.append(pl.BlockSpec((None, ka, tn), lambda j, i, b=row // ka: (layer, b, j)))
        args.append(w)
        row += ka
    assert row == w.shape[1]
    if res is not None:
        in_specs.append(pl.BlockSpec((tm, tn), lambda j, i: (i, j)))
        args.append(res)
    return pl.pallas_call(
        functools.partial(_mm_kernel, n_lhs=len(lhs), has_res=res is not None),
        grid=(n // tn, m // tm),
        in_specs=in_specs,
        out_specs=pl.BlockSpec((tm, tn), lambda j, i: (i, j)),
        out_shape=jax.ShapeDtypeStruct((m, n), out_dtype),
        compiler_params=_params("parallel", "parallel"),
        name=name,
    )(*args)


def _mem_attn_kernel(q_ref, k_ref, v_ref, gq_ref, gk_ref, o_ref, *, heads, hd):
    gq = gq_ref[...]
    gk = gk_ref[...]
    inv_sqrt = 1.0 / math.sqrt(hd)
    for h in range(heads):
        sl = slice(h * hd, (h + 1) * hd)
        q = q_ref[:, sl]
        k = k_ref[:, sl]
        qn = (q * lax.rsqrt(jnp.mean(q * q, axis=-1, keepdims=True) + EPS) * gq).astype(BF16)
        kn = (k * lax.rsqrt(jnp.mean(k * k, axis=-1, keepdims=True) + EPS) * gk).astype(BF16)
        logits = lax.dot_general(qn, kn, (((1,), (1,)), ((), ())),
                                 preferred_element_type=F32) * inv_sqrt
        mx = jnp.max(logits, axis=-1, keepdims=True)
        p = jnp.exp(logits - mx)
        p = p / jnp.sum(p, axis=-1, keepdims=True)
        o = jnp.dot(p.astype(BF16), v_ref[:, sl].astype(BF16), preferred_element_type=F32)
        o_ref[:, sl] = _pack_rows(o)


def mem_attention(proj, kv_mem, gq, gk, *, bsz, q_col_block, ts=512):
    t = proj.shape[0]
    seq = t // bsz
    mem_w = kv_mem.shape[1] // 2
    mem_tokens = kv_mem.shape[0] // bsz
    hd = mem_w // MEM_HEADS
    ts = min(ts, seq)
    nt = seq // ts
    return pl.pallas_call(
        functools.partial(_mem_attn_kernel, heads=MEM_HEADS, hd=hd),
        grid=(bsz, nt),
        in_specs=[pl.BlockSpec((ts, mem_w), lambda b, i: (b * nt + i, q_col_block)),
                  pl.BlockSpec((mem_tokens, mem_w), lambda b, i: (b, 0)),
                  pl.BlockSpec((mem_tokens, mem_w), lambda b, i: (b, 1)),
                  pl.BlockSpec((1, hd), lambda b, i: (0, 0)),
                  pl.BlockSpec((1, hd), lambda b, i: (0, 0))],
        out_specs=pl.BlockSpec((ts // 2, mem_w), lambda b, i: (b * nt + i, 0)),
        out_shape=jax.ShapeDtypeStruct((t // 2, mem_w), U32),
        compiler_params=_params("parallel", "parallel"),
        name="mem_attention",
    )(proj, kv_mem, kv_mem, gq.reshape(1, hd), gk.reshape(1, hd))


def _half_sign(shape, first, second):
    lane = lax.broadcasted_iota(jnp.int32, shape, len(shape) - 1)
    return jnp.where(lane < SSM_STATE, first, second).astype(F32)


def _swap_halves(x):
    return jnp.concatenate([x[:, SSM_STATE:], x[:, :SSM_STATE]], axis=1)


def _s5_prep_kernel(lr_ref, li_ref, ls_ref, b2_ref, c2_ref,
                    wc_ref, woc_ref, dd_ref, dec_ref, *, n_dec):
    L = S5_CHUNK
    lr = jnp.minimum(lr_ref[...], -1e-4)
    li = li_ref[...]
    step = jnp.exp(ls_ref[...])
    mag = jnp.exp(lr * step)
    a_re = mag * jnp.cos(li * step)
    a_im = mag * jnp.sin(li * step)
    p_re = a_re - 1.0
    den = lr * lr + li * li
    f_re = (p_re * lr + a_im * li) / den
    f_im = (a_im * lr - p_re * li) / den
    shape = lr.shape
    b2 = b2_ref[...]
    bb2 = f_re * b2 + f_im * (_swap_halves(b2) * _half_sign(shape, -1.0, 1.0))
    bb2s = _swap_halves(bb2) * _half_sign(shape, -1.0, 1.0)
    c2 = c2_ref[...]
    c_a = c2 * _half_sign(shape, 1.0, -1.0)
    c_b = -_swap_halves(c2)
    pw = [(jnp.ones(shape, F32), jnp.zeros(shape, F32))]
    for _ in range(L):
        pr, pi = pw[-1]
        pw.append((pr * a_re - pi * a_im, pr * a_im + pi * a_re))
    row_g = lax.broadcasted_iota(jnp.int32, shape, 0) // SSM_GROUP
    col_g = lax.broadcasted_iota(jnp.int32, shape, 1) // SSM_GROUP
    same_group = row_g == col_g
    for n in range(L):
        ar, ai = pw[L - 1 - n]
        wc_ref[0, n] = (ar * bb2 + ai * bb2s).astype(wc_ref.dtype)
        ar, ai = pw[n + 1]
        woc_ref[0, n] = (ar * c_a + ai * c_b).astype(woc_ref.dtype)
        ar, ai = pw[n]
        w0 = ar * c_a + ai * c_b
        kk = lax.dot_general(bb2, w0, (((1,), (1,)), ((), ())),
                             precision=lax.Precision.HIGHEST, preferred_element_type=F32)
        dd_ref[0, n] = jnp.where(same_group, kk, 0.0).astype(dd_ref.dtype)
    sgn = _half_sign(shape, -1.0, 1.0)
    pr, pi = pw[L]
    for k in range(n_dec):
        dec_ref[0, k] = pr
        dec_ref[0, n_dec + k] = pi * sgn
        pr, pi = pr * pr - pi * pi, 2.0 * pr * pi


def _shift_rows(x, sh):
    rows = lax.broadcasted_iota(jnp.int32, x.shape, 0)
    return jnp.where(rows >= sh, pltpu.roll(x, sh, 0), 0.0)


def _s5_main_kernel(u_ref, wc_ref, woc_ref, dd_ref, dec_ref, d_ref, y_ref,
                    win_s, wot_s, tt_s, *, bsz, seq, n_dec):
    L, H, GB = S5_CHUNK, SSM_GROUP, S5_BLOCK_GROUPS
    nc = seq // L

    @pl.when(pl.program_id(0) == 0)
    def _():
        win_s[...] = jnp.zeros(win_s.shape, win_s.dtype)
        wot_s[...] = jnp.zeros(wot_s.shape, wot_s.dtype)
        tt_s[...] = jnp.zeros(tt_s.shape, tt_s.dtype)

    for n in range(L):
        for g in range(GB):
            rows = slice(n * LANES + g * H, n * LANES + (g + 1) * H)
            cols = slice(g * LANES, (g + 1) * LANES)
            win_s[rows, cols] = wc_ref[0, n, g * H:(g + 1) * H, :]
            wot_s[rows, cols] = woc_ref[0, n, g * H:(g + 1) * H, :]
        for t in range(n, L):
            tt_s[n * LANES:(n + 1) * LANES, t * LANES:(t + 1) * LANES] = dd_ref[0, t - n]

    def piece(b, t):
        return u_ref[pl.ds(b * seq + t, nc, stride=L), :]

    ub = jnp.concatenate(
        [jnp.concatenate([piece(b, t).astype(BF16) for t in range(L)], axis=1)
         for b in range(bsz)], axis=0)
    z = jnp.dot(ub, win_s[...], preferred_element_type=F32)
    states = []
    for b in range(bsz):
        per_group = []
        for g in range(GB):
            x = z[b * nc:(b + 1) * nc, g * LANES:(g + 1) * LANES]
            for k in range(n_dec):
                if (1 << k) >= nc:
                    break
                xs = _shift_rows(x, 1 << k)
                a1 = dec_ref[0, k, g * H:g * H + 1, :]
                a2 = dec_ref[0, n_dec + k, g * H:g * H + 1, :]
                x = x + a1 * xs + a2 * pltpu.roll(xs, SSM_STATE, 1)
            per_group.append(_shift_rows(x, 1))
        states.append(jnp.concatenate(per_group, axis=1))
    s = jnp.concatenate(states, axis=0).astype(BF16)
    y = lax.dot_general(s, wot_s[...], (((1,), (1,)), ((), ())), preferred_element_type=F32)
    d_row = d_ref[...]
    tile_pairs = L * LANES // 256
    for j in range(tile_pairs):
        kk = (j + 1) * 256
        yj = y[:, j * 256:(j + 1) * 256] + jnp.dot(
            ub[:, :kk], tt_s[0:kk, j * 256:(j + 1) * 256], preferred_element_type=F32)
        for b in range(bsz):
            for tt in range(2):
                t = 2 * j + tt
                y_ref[pl.ds(b * seq + t, nc, stride=L), :] = (
                    yj[b * nc:(b + 1) * nc, tt * LANES:(tt + 1) * LANES] + d_row * piece(b, t))


def s5_scan(proj, lam_re, lam_im, log_step, b_re, b_im, c_re, c_im, d_skip, *, bsz):
    t = proj.shape[0]
    seq = t // bsz
    g = lam_re.shape[0]
    L, H, P, GB = S5_CHUNK, SSM_GROUP, SSM_STATE, S5_BLOCK_GROUPS
    nb = g // GB
    nc = seq // L
    n_dec = max(1, (nc - 1).bit_length())
    rows = lambda a: jnp.repeat(jnp.concatenate([a, a], axis=-1), H, axis=0)
    lr = rows(lam_re)
    li = rows(lam_im)
    ls = jnp.repeat(jnp.broadcast_to(log_step[:, None], (g, 2 * P)), H, axis=0)
    b2 = jnp.concatenate([jnp.swapaxes(b_re, 1, 2), jnp.swapaxes(b_im, 1, 2)],
                         axis=-1).reshape(g * H, 2 * P)
    c2 = jnp.concatenate([c_re, c_im], axis=-1).reshape(g * H, 2 * P)
    blk = pl.BlockSpec((LANES, 2 * P), lambda i: (i, 0))
    tab = pl.BlockSpec((1, L, LANES, LANES), lambda i: (i, 0, 0, 0))
    dec_spec = pl.BlockSpec((1, 2 * n_dec, LANES, LANES), lambda i: (i, 0, 0, 0))
    wc, woc, dd, dec = pl.pallas_call(
        functools.partial(_s5_prep_kernel, n_dec=n_dec),
        grid=(nb,),
        in_specs=[blk, blk, blk, blk, blk],
        out_specs=[tab, tab, tab, dec_spec],
        out_shape=[jax.ShapeDtypeStruct((nb, L, LANES, LANES), BF16),
                   jax.ShapeDtypeStruct((nb, L, LANES, LANES), BF16),
                   jax.ShapeDtypeStruct((nb, L, LANES, LANES), BF16),
                   jax.ShapeDtypeStruct((nb, 2 * n_dec, LANES, LANES), F32)],
        compiler_params=_params("parallel"),
        name="s5_prep",
    )(lr, li, ls, b2, c2)
    return pl.pallas_call(
        functools.partial(_s5_main_kernel, bsz=bsz, seq=seq, n_dec=n_dec),
        grid=(nb,),
        in_specs=[pl.BlockSpec((t, LANES), lambda i: (0, i)),
                  tab, tab, tab, dec_spec,
                  pl.BlockSpec((1, LANES), lambda i: (0, i))],
        out_specs=pl.BlockSpec((t, LANES), lambda i: (0, i)),
        out_shape=jax.ShapeDtypeStruct((t, g * H), F32),
        scratch_shapes=[pltpu.VMEM((L * LANES, GB * LANES), BF16),
                        pltpu.VMEM((L * LANES, GB * LANES), BF16),
                        pltpu.VMEM((L * LANES, L * LANES), BF16)],
        compiler_params=_params("arbitrary"),
        name="s5_main",
    )(proj, wc, woc, dd, dec, d_skip.reshape(1, g * H))


def _glu_kernel(y_ref, w_ref, o_ref):
    y = y_ref[...]
    c = math.sqrt(2.0 / math.pi)
    g = 0.5 * y * (1.0 + jnp.tanh(c * (y + 0.044715 * (y * y * y))))
    z = jnp.dot(g.astype(BF16), w_ref[...].astype(BF16), preferred_element_type=F32)
    o_ref[...] = _pack_rows(g * (1.0 / (1.0 + jnp.exp(-z))))


def gelu_glu(y, w_glu, layer, tm=512):
    m, n = y.shape
    tm = min(tm, m)
    return pl.pallas_call(
        _glu_kernel,
        grid=(m // tm,),
        in_specs=[pl.BlockSpec((tm, n), lambda i: (i, 0)),
                  pl.BlockSpec((None, n, n), lambda i: (layer, 0, 0))],
        out_specs=pl.BlockSpec((tm // 2, n), lambda i: (i, 0)),
        out_shape=jax.ShapeDtypeStruct((m // 2, n), U32),
        compiler_params=_params("parallel"),
        name="gelu_glu",
    )(y, w_glu)


def _sb_kernel(q_ref, k_ref, v_ref, o_ref, *, tq, hd, heads_per_step, scale):
    i = pl.program_id(2)
    row = lax.broadcasted_iota(jnp.int32, (tq, tq), 0)
    col = lax.broadcasted_iota(jnp.int32, (tq, tq), 1)
    later = jnp.where(row > col, 1.0, 0.0).astype(BF16)
    causal = col < row
    qs = [q_ref[:, h * hd:(h + 1) * hd].astype(BF16) for h in range(heads_per_step)]

    def block(h, j, carry, diagonal):
        acc, rem = carry
        start = pl.multiple_of(j * tq, tq)
        kj = k_ref[pl.ds(start, tq), h * hd:(h + 1) * hd]
        vj = v_ref[pl.ds(start, tq), h * hd:(h + 1) * hd]
        z = lax.dot_general(qs[h], kj, (((1,), (1,)), ((), ())),
                            preferred_element_type=F32) * scale
        sp = jnp.maximum(z, 0.0) + jnp.log(1.0 + jnp.exp(-jnp.abs(z)))
        spm = jnp.where(causal, sp, 0.0) if diagonal else sp
        hi = spm.astype(BF16)
        lo = (spm - hi.astype(F32)).astype(BF16)
        cs = (jnp.dot(hi, later, preferred_element_type=F32)
              + jnp.dot(lo, later, preferred_element_type=F32))
        w = jnp.exp(z - sp - cs + rem)
        if diagonal:
            w = jnp.where(causal, w, 0.0)
        acc = acc + jnp.dot(w.astype(BF16), vj, preferred_element_type=F32)
        rem = rem - jnp.sum(spm, axis=1, keepdims=True)
        return acc, rem

    init = (jnp.zeros((tq, hd), F32), jnp.zeros((tq, 1), F32))
    heads = range(heads_per_step)

    def diagonal_only():
        return [block(h, i, init, True) for h in heads]

    def diagonal_and_previous():
        return [block(h, i - 1, block(h, i, init, True), False) for h in heads]

    carries = lax.cond(i > 0, diagonal_and_previous, diagonal_only)

    for h in heads:
        def more(c):
            n, _, rem = c
            return jnp.logical_and(n < i, jnp.max(rem) > -F32_EXP_UNDERFLOW)

        def step(c, h=h):
            n, acc, rem = c
            acc, rem = block(h, i - 1 - n, (acc, rem), False)
            return n + 1, acc, rem

        _, acc, _ = lax.while_loop(more, step, (jnp.int32(1), *carries[h]))
        o_ref[:, h * hd:(h + 1) * hd] = _pack_rows(acc)


def stick_breaking(proj, kv, *, bsz, heads, tq=256, heads_per_step=4):
    t = proj.shape[0]
    seq = t // bsz
    hd = SB_HEAD_DIM
    tq = min(tq, seq)
    nq = seq // tq
    hg = heads // heads_per_step
    w = heads_per_step * hd
    return pl.pallas_call(
        functools.partial(_sb_kernel, tq=tq, hd=hd, heads_per_step=heads_per_step,
                          scale=1.0 / math.sqrt(hd)),
        grid=(bsz, hg, nq),
        in_specs=[pl.BlockSpec((tq, w), lambda b, h, i: (b * nq + i, h)),
                  pl.BlockSpec((seq, w), lambda b, h, i: (b, h)),
                  pl.BlockSpec((seq, w), lambda b, h, i: (b, hg + h))],
        out_specs=pl.BlockSpec((tq // 2, w), lambda b, h, i: (b * nq + i, h)),
        out_shape=jax.ShapeDtypeStruct((t // 2, heads * hd), U32),
        compiler_params=_params("parallel", "parallel", "arbitrary"),
        name="stick_breaking",
    )(proj, kv, kv)


def _causal_conv(up, tail, cw_ref, cb_ref):
    r8 = lax.broadcasted_iota(jnp.int32, tail.shape, 0)
    acc = up * cw_ref[CONV_W - 1:CONV_W, :] + cb_ref[...]
    for sh in range(1, CONV_W):
        rolled = pltpu.roll(up, sh, 0)
        top = jnp.where(r8 < sh, pltpu.roll(tail, sh, 0), rolled[0:SUBLANES])
        shifted = jnp.concatenate([top, rolled[SUBLANES:]], axis=0)
        acc = acc + shifted * cw_ref[CONV_W - 1 - sh:CONV_W - sh, :]
    return acc


def _ffn_up_kernel(x_ref, wa_ref, wb_ref, cwa_ref, cwb_ref, cba_ref, cbb_ref, wd_ref,
                   o_ref, wd_bf16_ref, haloa_ref, halob_ref, *, tm, sub, tiles_per_seq):
    i = pl.program_id(0)
    j = pl.program_id(1)

    @pl.when(i == 0)
    def _():
        wd_bf16_ref[...] = wd_ref[...].astype(BF16)

    @pl.when(i % tiles_per_seq == 0)
    def _():
        haloa_ref[j] = jnp.zeros(haloa_ref.shape[1:], F32)
        halob_ref[j] = jnp.zeros(halob_ref.shape[1:], F32)

    tail_a = haloa_ref[j]
    tail_b = halob_ref[j]
    wa = wa_ref[...].astype(BF16)
    wb = wb_ref[...].astype(BF16)
    for r0 in range(0, tm, sub):
        x = _unpack_rows(x_ref[r0 // 2:(r0 + sub) // 2, :])
        up_a = jnp.dot(x, wa, preferred_element_type=F32)
        up_b = jnp.dot(x, wb, preferred_element_type=F32)
        a = _causal_conv(up_a, tail_a, cwa_ref, cba_ref)
        b = _causal_conv(up_b, tail_b, cwb_ref, cbb_ref)
        tail_a = up_a[sub - SUBLANES:]
        tail_b = up_b[sub - SUBLANES:]
        o_ref[r0 // 2:(r0 + sub) // 2, :] = _pack_rows(a * (1.0 / (1.0 + jnp.exp(-a))) * b)
    haloa_ref[j] = tail_a
    halob_ref[j] = tail_b


def ffn_up(xn, w_up, conv_w, conv_b, w_down, layer, *, seq, tm=2048, tf=256, sub=128):
    t = 2 * xn.shape[0]
    d = xn.shape[1]
    ff = w_up.shape[2] // 2
    tm = min(tm, seq)
    sub = min(sub, tm)
    nf = ff // tf
    cb = conv_b.reshape(conv_b.shape[0], 1, 2 * ff)
    first_tile_chunk = lambda i, j: jnp.where(i == 0, j, nf - 1)
    return pl.pallas_call(
        functools.partial(_ffn_up_kernel, tm=tm, sub=sub, tiles_per_seq=seq // tm),
        grid=(t // tm, nf),
        in_specs=[pl.BlockSpec((tm // 2, d), lambda i, j: (i, 0), pipeline_mode=pl.Buffered(1)),
                  pl.BlockSpec((None, d, tf), lambda i, j: (layer, 0, j)),
                  pl.BlockSpec((None, d, tf), lambda i, j: (layer, 0, nf + j)),
                  pl.BlockSpec((None, CONV_W, tf), lambda i, j: (layer, 0, j)),
                  pl.BlockSpec((None, CONV_W, tf), lambda i, j: (layer, 0, nf + j)),
                  pl.BlockSpec((None, 1, tf), lambda i, j: (layer, 0, j)),
                  pl.BlockSpec((None, 1, tf), lambda i, j: (layer, 0, nf + j)),
                  pl.BlockSpec((None, tf, d), lambda i, j: (layer, first_tile_chunk(i, j), 0))],
        out_specs=[pl.BlockSpec((tm // 2, tf), lambda i, j: (i, j)),
                   pl.BlockSpec((None, tf, d), lambda i, j: (0, first_tile_chunk(i, j), 0))],
        out_shape=[jax.ShapeDtypeStruct((t // 2, ff), U32),
                   jax.ShapeDtypeStruct((1, ff, d), BF16)],
        scratch_shapes=[pltpu.VMEM((nf, SUBLANES, tf), F32),
                        pltpu.VMEM((nf, SUBLANES, tf), F32)],
        compiler_params=_params("arbitrary", "arbitrary"),
        name="ffn_up",
    )(xn, w_up, w_up, conv_w, conv_w, cb, cb, w_down)


def kernel(x, mem, norm_mix_g, w_in, w_out, mem_norm_g, w_mem_kv, mem_q_norm_g, mem_k_norm_g,
           s5_lam_re, s5_lam_im, s5_log_step, s5_b_re, s5_b_im, s5_c_re, s5_c_im, s5_d,
           s5_w_glu, kv_norm_g, w_kv_shared, norm_ffn_g, w_ffn_up, ffn_conv_w, ffn_conv_b,
           w_ffn_down):
    bsz, seq, d = x.shape
    depth = w_in.shape[0]
    n_a = s5_lam_re.shape[0]
    mem2 = mem.reshape(bsz * mem.shape[1], d)
    ssm_w = s5_w_glu.shape[1]
    mem_w = w_mem_kv.shape[2] // 2
    h = x.reshape(bsz * seq, d)
    kv_sh = None
    for i in range(depth):
        proj = matmul([rmsnorm(h, norm_mix_g[i])], w_in, i, tm=512, tn=1024, name="w_in")
        kv_mem = matmul([rmsnorm(mem2, mem_norm_g[i])], w_mem_kv, i, name="w_mem_kv")
        m_out = mem_attention(proj, kv_mem, mem_q_norm_g[i], mem_k_norm_g[i],
                              bsz=bsz, q_col_block=ssm_w // mem_w)
        if i < n_a:
            y = s5_scan(proj, s5_lam_re[i], s5_lam_im[i], s5_log_step[i], s5_b_re[i],
                        s5_b_im[i], s5_c_re[i], s5_c_im[i], s5_d[i], bsz=bsz)
            p_out = gelu_glu(y, s5_w_glu, i)
        else:
            p_out = stick_breaking(proj, kv_sh, bsz=bsz, heads=ssm_w // SB_HEAD_DIM)
        h = matmul([p_out, m_out], w_out, i, res=h, tm=512, tn=1024, name="w_out")
        act, w_down = ffn_up(rmsnorm(h, norm_ffn_g[i]), w_ffn_up, ffn_conv_w, ffn_conv_b,
                             w_ffn_down, i, seq=seq)
        h = matmul([act], w_down, 0, res=h, tm=512, tn=512, name="ffn_down")
        if i == n_a - 1:
            kv_sh = matmul([rmsnorm(h, kv_norm_g)], w_kv_shared[None], 0, out_dtype=BF16,
                           tm=512, tn=1024, name="w_kv_shared")
    return h.reshape(bsz, seq, d)
```

```python
import functools
import math

import jax
import jax.numpy as jnp
from jax import lax
from jax.experimental import pallas as pl
from jax.experimental.pallas import tpu as pltpu

F32 = jnp.float32
BF16 = jnp.bfloat16
U32 = jnp.uint32
EPS = 1e-6

V7X_VMEM_BYTES = 64 * 1024 * 1024
VMEM_LIMIT_BYTES = V7X_VMEM_BYTES - 8 * 1024 * 1024
SUBLANES = 8
LANES = 128

SSM_GROUP = 16
SSM_STATE = 64
S5_CHUNK = 16
S5_BLOCK_GROUPS = LANES // SSM_GROUP
SB_HEAD_DIM = 128
MEM_HEADS = 4
CONV_W = 3
F32_EXP_UNDERFLOW = 104.0


def _params(*semantics):
    return pltpu.CompilerParams(dimension_semantics=semantics,
                                vmem_limit_bytes=VMEM_LIMIT_BYTES)


def _pack_rows(y):
    return pltpu.bitcast(y.astype(BF16), U32)


def _unpack_rows(p):
    return pltpu.bitcast(p, BF16)


def _rms_kernel(x_ref, g_ref, o_ref):
    x = x_ref[...]
    ms = jnp.mean(x * x, axis=-1, keepdims=True)
    o_ref[...] = _pack_rows(x * lax.rsqrt(ms + EPS) * g_ref[...])


def rmsnorm(x, g, tm=512):
    m, d = x.shape
    tm = min(tm, m)
    return pl.pallas_call(
        _rms_kernel,
        grid=(m // tm,),
        in_specs=[pl.BlockSpec((tm, d), lambda i: (i, 0)),
                  pl.BlockSpec((1, d), lambda i: (0, 0))],
        out_specs=pl.BlockSpec((tm // 2, d), lambda i: (i, 0)),
        out_shape=jax.ShapeDtypeStruct((m // 2, d), U32),
        compiler_params=_params("parallel"),
        name="rmsnorm",
    )(x, g.reshape(1, d))


def _mm_kernel(*refs, n_lhs, has_res):
    a_refs = refs[:n_lhs]
    w_refs = refs[n_lhs:2 * n_lhs]
    r_ref = refs[2 * n_lhs] if has_res else None
    o_ref = refs[-1]
    acc = None
    for a_ref, w_ref in zip(a_refs, w_refs):
        d = jnp.dot(_unpack_rows(a_ref[...]), w_ref[...].astype(BF16),
                    preferred_element_type=F32)
        acc = d if acc is None else acc + d
    if has_res:
        acc = r_ref[...] + acc
    o_ref[...] = acc.astype(o_ref.dtype)


def matmul(lhs, w, layer, res=None, out_dtype=F32, tm=1024, tn=512, name="matmul"):
    m = 2 * lhs[0].shape[0]
    n = w.shape[2]
    tm = min(tm, m)
    tn = min(tn, n)
    in_specs, args = [], []
    for a in lhs:
        in_specs.append(pl.BlockSpec((tm // 2, a.shape[1]), lambda j, i: (i, 0)))
        args.append(a)
    row = 0
    for a in lhs:
        ka = a.shape[1]
        assert row % ka == 0
        in_specs.append(pl.BlockSpec((None, ka, tn), lambda j, i, b=row // ka: (layer, b, j)))
        args.append(w)
        row += ka
    assert row == w.shape[1]
    if res is not None:
        in_specs.append(pl.BlockSpec((tm, tn), lambda j, i: (i, j)))
        args.append(res)
    return pl.pallas_call(
        functools.partial(_mm_kernel, n_lhs=len(lhs), has_res=res is not None),
        grid=(n // tn, m // tm),
        in_specs=in_specs,
        out_specs=pl.BlockSpec((tm, tn), lambda j, i: (i, j)),
        out_shape=jax.ShapeDtypeStruct((m, n), out_dtype),
        compiler_params=_params("parallel", "parallel"),
        name=name,
    )(*args)


def _mem_attn_kernel(q_ref, k_ref, v_ref, gq_ref, gk_ref, o_ref, *, heads, hd):
    gq = gq_ref[...]
    gk = gk_ref[...]
    inv_sqrt = 1.0 / math.sqrt(hd)
    for h in range(heads):
        sl = slice(h * hd, (h + 1) * hd)
        q = q_ref[:, sl]
        k = k_ref[:, sl]
        qn = (q * lax.rsqrt(jnp.mean(q * q, axis=-1, keepdims=True) + EPS) * gq).astype(BF16)
        kn = (k * lax.rsqrt(jnp.mean(k * k, axis=-1, keepdims=True) + EPS) * gk).astype(BF16)
        logits = lax.dot_general(qn, kn, (((1,), (1,)), ((), ())),
                                 preferred_element_type=F32) * inv_sqrt
        mx = jnp.max(logits, axis=-1, keepdims=True)
        p = jnp.exp(logits - mx)
        p = p / jnp.sum(p, axis=-1, keepdims=True)
        o = jnp.dot(p.astype(BF16), v_ref[:, sl].astype(BF16), preferred_element_type=F32)
        o_ref[:, sl] = _pack_rows(o)


def mem_attention(proj, kv_mem, gq, gk, *, bsz, q_col_block, ts=512):
    t = proj.shape[0]
    seq = t // bsz
    mem_w = kv_mem.shape[1] // 2
    mem_tokens = kv_mem.shape[0] // bsz
    hd = mem_w // MEM_HEADS
    ts = min(ts, seq)
    nt = seq // ts
    return pl.pallas_call(
        functools.partial(_mem_attn_kernel, heads=MEM_HEADS, hd=hd),
        grid=(bsz, nt),
        in_specs=[pl.BlockSpec((ts, mem_w), lambda b, i: (b * nt + i, q_col_block)),
                  pl.BlockSpec((mem_tokens, mem_w), lambda b, i: (b, 0)),
                  pl.BlockSpec((mem_tokens, mem_w), lambda b, i: (b, 1)),
                  pl.BlockSpec((1, hd), lambda b, i: (0, 0)),
                  pl.BlockSpec((1, hd), lambda b, i: (0, 0))],
        out_specs=pl.BlockSpec((ts // 2, mem_w), lambda b, i: (b * nt + i, 0)),
        out_shape=jax.ShapeDtypeStruct((t // 2, mem_w), U32),
        compiler_params=_params("parallel", "parallel"),
        name="mem_attention",
    )(proj, kv_mem, kv_mem, gq.reshape(1, hd), gk.reshape(1, hd))


def _half_sign(shape, first, second):
    lane = lax.broadcasted_iota(jnp.int32, shape, len(shape) - 1)
    return jnp.where(lane < SSM_STATE, first, second).astype(F32)


def _swap_halves(x):
    return jnp.concatenate([x[:, SSM_STATE:], x[:, :SSM_STATE]], axis=1)


def _s5_prep_kernel(lr_ref, li_ref, ls_ref, b2_ref, c2_ref,
                    wc_ref, woc_ref, dd_ref, dec_ref, *, n_dec):
    L = S5_CHUNK
    lr = jnp.minimum(lr_ref[...], -1e-4)
    li = li_ref[...]
    step = jnp.exp(ls_ref[...])
    mag = jnp.exp(lr * step)
    a_re = mag * jnp.cos(li * step)
    a_im = mag * jnp.sin(li * step)
    p_re = a_re - 1.0
    den = lr * lr + li * li
    f_re = (p_re * lr + a_im * li) / den
    f_im = (a_im * lr - p_re * li) / den
    shape = lr.shape
    b2 = b2_ref[...]
    bb2 = f_re * b2 + f_im * (_swap_halves(b2) * _half_sign(shape, -1.0, 1.0))
    bb2s = _swap_halves(bb2) * _half_sign(shape, -1.0, 1.0)
    c2 = c2_ref[...]
    c_a = c2 * _half_sign(shape, 1.0, -1.0)
    c_b = -_swap_halves(c2)
    pw = [(jnp.ones(shape, F32), jnp.zeros(shape, F32))]
    for _ in range(L):
        pr, pi = pw[-1]
        pw.append((pr * a_re - pi * a_im, pr * a_im + pi * a_re))
    row_g = lax.broadcasted_iota(jnp.int32, shape, 0) // SSM_GROUP
    col_g = lax.broadcasted_iota(jnp.int32, shape, 1) // SSM_GROUP
    same_group = row_g == col_g
    for n in range(L):
        ar, ai = pw[L - 1 - n]
        wc_ref[0, n] = (ar * bb2 + ai * bb2s).astype(wc_ref.dtype)
        ar, ai = pw[n + 1]
        woc_ref[0, n] = (ar * c_a + ai * c_b).astype(woc_ref.dtype)
        ar, ai = pw[n]
        w0 = ar * c_a + ai * c_b
        kk = lax.dot_general(bb2, w0, (((1,), (1,)), ((), ())),
                             precision=lax.Precision.HIGHEST, preferred_element_type=F32)
        dd_ref[0, n] = jnp.where(same_group, kk, 0.0).astype(dd_ref.dtype)
    sgn = _half_sign(shape, -1.0, 1.0)
    pr, pi = pw[L]
    for k in range(n_dec):
        dec_ref[0, k] = pr
        dec_ref[0, n_dec + k] = pi * sgn
        pr, pi = pr * pr - pi * pi, 2.0 * pr * pi


def _shift_rows(x, sh):
    rows = lax.broadcasted_iota(jnp.int32, x.shape, 0)
    return jnp.where(rows >= sh, pltpu.roll(x, sh, 0), 0.0)


def _s5_main_kernel(u_ref, wc_ref, woc_ref, dd_ref, dec_ref, d_ref, y_ref,
                    win_s, wot_s, tt_s, *, bsz, seq, n_dec):
    L, H, GB = S5_CHUNK, SSM_GROUP, S5_BLOCK_GROUPS
    nc = seq // L

    @pl.when(pl.program_id(0) == 0)
    def _():
        win_s[...] = jnp.zeros(win_s.shape, win_s.dtype)
        wot_s[...] = jnp.zeros(wot_s.shape, wot_s.dtype)
        tt_s[...] = jnp.zeros(tt_s.shape, tt_s.dtype)

    for n in range(L):
        for g in range(GB):
            rows = slice(n * LANES + g * H, n * LANES + (g + 1) * H)
            cols = slice(g * LANES, (g + 1) * LANES)
            win_s[rows, cols] = wc_ref[0, n, g * H:(g + 1) * H, :]
            wot_s[rows, cols] = woc_ref[0, n, g * H:(g + 1) * H, :]
        for t in range(n, L):
            tt_s[n * LANES:(n + 1) * LANES, t * LANES:(t + 1) * LANES] = dd_ref[0, t - n]

    def piece(b, t):
        return u_ref[pl.ds(b * seq + t, nc, stride=L), :]

    ub = jnp.concatenate(
        [jnp.concatenate([piece(b, t).astype(BF16) for t in range(L)], axis=1)
         for b in range(bsz)], axis=0)
    z = jnp.dot(ub, win_s[...], preferred_element_type=F32)
    states = []
    for b in range(bsz):
        per_group = []
        for g in range(GB):
            x = z[b * nc:(b + 1) * nc, g * LANES:(g + 1) * LANES]
            for k in range(n_dec):
                if (1 << k) >= nc:
                    break
                xs = _shift_rows(x, 1 << k)
                a1 = dec_ref[0, k, g * H:g * H + 1, :]
                a2 = dec_ref[0, n_dec + k, g * H:g * H + 1, :]
                x = x + a1 * xs + a2 * pltpu.roll(xs, SSM_STATE, 1)
            per_group.append(_shift_rows(x, 1))
        states.append(jnp.concatenate(per_group, axis=1))
    s = jnp.concatenate(states, axis=0).astype(BF16)
    y = lax.dot_general(s, wot_s[...], (((1,), (1,)), ((), ())), preferred_element_type=F32)
    d_row = d_ref[...]
    tile_pairs = L * LANES // 256
    for j in range(tile_pairs):
        kk = (j + 1) * 256
        yj = y[:, j * 256:(j + 1) * 256] + jnp.dot(
            ub[:, :kk], tt_s[0:kk, j * 256:(j + 1) * 256], preferred_element_type=F32)
        for b in range(bsz):
            for tt in range(2):
                t = 2 * j + tt
                y_ref[pl.ds(b * seq + t, nc, stride=L), :] = (
                    yj[b * nc:(b + 1) * nc, tt * LANES:(tt + 1) * LANES] + d_row * piece(b, t))


def s5_scan(proj, lam_re, lam_im, log_step, b_re, b_im, c_re, c_im, d_skip, *, bsz):
    t = proj.shape[0]
    seq = t // bsz
    g = lam_re.shape[0]
    L, H, P, GB = S5_CHUNK, SSM_GROUP, SSM_STATE, S5_BLOCK_GROUPS
    nb = g // GB
    nc = seq // L
    n_dec = max(1, (nc - 1).bit_length())
    rows = lambda a: jnp.repeat(jnp.concatenate([a, a], axis=-1), H, axis=0)
    lr = rows(lam_re)
    li = rows(lam_im)
    ls = jnp.repeat(jnp.broadcast_to(log_step[:, None], (g, 2 * P)), H, axis=0)
    b2 = jnp.concatenate([jnp.swapaxes(b_re, 1, 2), jnp.swapaxes(b_im, 1, 2)],
                         axis=-1).reshape(g * H, 2 * P)
    c2 = jnp.concatenate([c_re, c_im], axis=-1).reshape(g * H, 2 * P)
    blk = pl.BlockSpec((LANES, 2 * P), lambda i: (i, 0))
    tab = pl.BlockSpec((1, L, LANES, LANES), lambda i: (i, 0, 0, 0))
    dec_spec = pl.BlockSpec((1, 2 * n_dec, LANES, LANES), lambda i: (i, 0, 0, 0))
    wc, woc, dd, dec = pl.pallas_call(
        functools.partial(_s5_prep_kernel, n_dec=n_dec),
        grid=(nb,),
        in_specs=[blk, blk, blk, blk, blk],
        out_specs=[tab, tab, tab, dec_spec],
        out_shape=[jax.ShapeDtypeStruct((nb, L, LANES, LANES), BF16),
                   jax.ShapeDtypeStruct((nb, L, LANES, LANES), BF16),
                   jax.ShapeDtypeStruct((nb, L, LANES, LANES), BF16),
                   jax.ShapeDtypeStruct((nb, 2 * n_dec, LANES, LANES), F32)],
        compiler_params=_params("parallel"),
        name="s5_prep",
    )(lr, li, ls, b2, c2)
    return pl.pallas_call(
        functools.partial(_s5_main_kernel, bsz=bsz, seq=seq, n_dec=n_dec),
        grid=(nb,),
        in_specs=[pl.BlockSpec((t, LANES), lambda i: (0, i)),
                  tab, tab, tab, dec_spec,
                  pl.BlockSpec((1, LANES), lambda i: (0, i))],
        out_specs=pl.BlockSpec((t, LANES), lambda i: (0, i)),
        out_shape=jax.ShapeDtypeStruct((t, g * H), F32),
        scratch_shapes=[pltpu.VMEM((L * LANES, GB * LANES), BF16),
                        pltpu.VMEM((L * LANES, GB * LANES), BF16),
                        pltpu.VMEM((L * LANES, L * LANES), BF16)],
        compiler_params=_params("arbitrary"),
        name="s5_main",
    )(proj, wc, woc, dd, dec, d_skip.reshape(1, g * H))


def _glu_kernel(y_ref, w_ref, o_ref):
    y = y_ref[...]
    c = math.sqrt(2.0 / math.pi)
    g = 0.5 * y * (1.0 + jnp.tanh(c * (y + 0.044715 * (y * y * y))))
    z = jnp.dot(g.astype(BF16), w_ref[...].astype(BF16), preferred_element_type=F32)
    o_ref[...] = _pack_rows(g * (1.0 / (1.0 + jnp.exp(-z))))


def gelu_glu(y, w_glu, layer, tm=512):
    m, n = y.shape
    tm = min(tm, m)
    return pl.pallas_call(
        _glu_kernel,
        grid=(m // tm,),
        in_specs=[pl.BlockSpec((tm, n), lambda i: (i, 0)),
                  pl.BlockSpec((None, n, n), lambda i: (layer, 0, 0))],
        out_specs=pl.BlockSpec((tm // 2, n), lambda i: (i, 0)),
        out_shape=jax.ShapeDtypeStruct((m // 2, n), U32),
        compiler_params=_params("parallel"),
        name="gelu_glu",
    )(y, w_glu)


def _sb_kernel(q_ref, k_ref, v_ref, o_ref, *, tq, hd, heads_per_step, scale):
    i = pl.program_id(2)
    row = lax.broadcasted_iota(jnp.int32, (tq, tq), 0)
    col = lax.broadcasted_iota(jnp.int32, (tq, tq), 1)
    later = jnp.where(row > col, 1.0, 0.0).astype(BF16)
    causal = col < row
    qs = [q_ref[:, h * hd:(h + 1) * hd].astype(BF16) for h in range(heads_per_step)]

    def block(h, j, carry, diagonal):
        acc, rem = carry
        start = pl.multiple_of(j * tq, tq)
        kj = k_ref[pl.ds(start, tq), h * hd:(h + 1) * hd]
        vj = v_ref[pl.ds(start, tq), h * hd:(h + 1) * hd]
        z = lax.dot_general(qs[h], kj, (((1,), (1,)), ((), ())),
                            preferred_element_type=F32) * scale
        sp = jnp.maximum(z, 0.0) + jnp.log(1.0 + jnp.exp(-jnp.abs(z)))
        spm = jnp.where(causal, sp, 0.0) if diagonal else sp
        hi = spm.astype(BF16)
        lo = (spm - hi.astype(F32)).astype(BF16)
        cs = (jnp.dot(hi, later, preferred_element_type=F32)
              + jnp.dot(lo, later, preferred_element_type=F32))
        w = jnp.exp(z - sp - cs + rem)
        if diagonal:
            w = jnp.where(causal, w, 0.0)
        acc = acc + jnp.dot(w.astype(BF16), vj, preferred_element_type=F32)
        rem = rem - jnp.sum(spm, axis=1, keepdims=True)
        return acc, rem

    init = (jnp.zeros((tq, hd), F32), jnp.zeros((tq, 1), F32))
    heads = range(heads_per_step)

    def diagonal_only():
        return [block(h, i, init, True) for h in heads]

    def diagonal_and_previous():
        return [block(h, i - 1, block(h, i, init, True), False) for h in heads]

    carries = lax.cond(i > 0, diagonal_and_previous, diagonal_only)

    for h in heads:
        def more(c):
            n, _, rem = c
            return jnp.logical_and(n < i, jnp.max(rem) > -F32_EXP_UNDERFLOW)

        def step(c, h=h):
            n, acc, rem = c
            acc, rem = block(h, i - 1 - n, (acc, rem), False)
            return n + 1, acc, rem

        _, acc, _ = lax.while_loop(more, step, (jnp.int32(1), *carries[h]))
        o_ref[:, h * hd:(h + 1) * hd] = _pack_rows(acc)


def stick_breaking(proj, kv, *, bsz, heads, tq=256, heads_per_step=8):
    t = proj.shape[0]
    seq = t // bsz
    hd = SB_HEAD_DIM
    tq = min(tq, seq)
    nq = seq // tq
    hg = heads // heads_per_step
    w = heads_per_step * hd
    return pl.pallas_call(
        functools.partial(_sb_kernel, tq=tq, hd=hd, heads_per_step=heads_per_step,
                          scale=1.0 / math.sqrt(hd)),
        grid=(bsz, hg, nq),
        in_specs=[pl.BlockSpec((tq, w), lambda b, h, i: (b * nq + i, h)),
                  pl.BlockSpec((seq, w), lambda b, h, i: (b, h)),
                  pl.BlockSpec((seq, w), lambda b, h, i: (b, hg + h))],
        out_specs=pl.BlockSpec((tq // 2, w), lambda b, h, i: (b * nq + i, h)),
        out_shape=jax.ShapeDtypeStruct((t // 2, heads * hd), U32),
        compiler_params=_params("parallel", "parallel", "arbitrary"),
        name="stick_breaking",
    )(proj, kv, kv)


def _causal_conv(up, tail, cw_ref, cb_ref):
    r8 = lax.broadcasted_iota(jnp.int32, tail.shape, 0)
    acc = up * cw_ref[CONV_W - 1:CONV_W, :] + cb_ref[...]
    for sh in range(1, CONV_W):
        rolled = pltpu.roll(up, sh, 0)
        top = jnp.where(r8 < sh, pltpu.roll(tail, sh, 0), rolled[0:SUBLANES])
        shifted = jnp.concatenate([top, rolled[SUBLANES:]], axis=0)
        acc = acc + shifted * cw_ref[CONV_W - 1 - sh:CONV_W - sh, :]
    return acc


def _ffn_up_kernel(x_ref, wa_ref, wb_ref, cwa_ref, cwb_ref, cba_ref, cbb_ref, wd_ref,
                   o_ref, wd_bf16_ref, haloa_ref, halob_ref, *, tm, sub, tiles_per_seq):
    i = pl.program_id(0)
    j = pl.program_id(1)

    @pl.when(i == 0)
    def _():
        wd_bf16_ref[...] = wd_ref[...].astype(BF16)

    @pl.when(i % tiles_per_seq == 0)
    def _():
        haloa_ref[j] = jnp.zeros(haloa_ref.shape[1:], F32)
        halob_ref[j] = jnp.zeros(halob_ref.shape[1:], F32)

    tail_a = haloa_ref[j]
    tail_b = halob_ref[j]
    wa = wa_ref[...].astype(BF16)
    wb = wb_ref[...].astype(BF16)
    for r0 in range(0, tm, sub):
        x = _unpack_rows(x_ref[r0 // 2:(r0 + sub) // 2, :])
        up_a = jnp.dot(x, wa, preferred_element_type=F32)
        up_b = jnp.dot(x, wb, preferred_element_type=F32)
        a = _causal_conv(up_a, tail_a, cwa_ref, cba_ref)
        b = _causal_conv(up_b, tail_b, cwb_ref, cbb_ref)
        tail_a = up_a[sub - SUBLANES:]
        tail_b = up_b[sub - SUBLANES:]
        o_ref[r0 // 2:(r0 + sub) // 2, :] = _pack_rows(a * (1.0 / (1.0 + jnp.exp(-a))) * b)
    haloa_ref[j] = tail_a
    halob_ref[j] = tail_b


def ffn_up(xn, w_up, conv_w, conv_b, w_down, layer, *, seq, tm=2048, tf=256, sub=128):
    t = 2 * xn.shape[0]
    d = xn.shape[1]
    ff = w_up.shape[2] // 2
    tm = min(tm, seq)
    sub = min(sub, tm)
    nf = ff // tf
    cb = conv_b.reshape(conv_b.shape[0], 1, 2 * ff)
    first_tile_chunk = lambda i, j: jnp.where(i == 0, j, nf - 1)
    return pl.pallas_call(
        functools.partial(_ffn_up_kernel, tm=tm, sub=sub, tiles_per_seq=seq // tm),
        grid=(t // tm, nf),
        in_specs=[pl.BlockSpec((tm // 2, d), lambda i, j: (i, 0), pipeline_mode=pl.Buffered(1)),
                  pl.BlockSpec((None, d, tf), lambda i, j: (layer, 0, j)),
                  pl.BlockSpec((None, d, tf), lambda i, j: (layer, 0, nf + j)),
                  pl.BlockSpec((None, CONV_W, tf), lambda i, j: (layer, 0, j)),
                  pl.BlockSpec((None, CONV_W, tf), lambda i, j: (layer, 0, nf + j)),
                  pl.BlockSpec((None, 1, tf), lambda i, j: (layer, 0, j)),
                  pl.BlockSpec((None, 1, tf), lambda i, j: (layer, 0, nf + j)),
                  pl.BlockSpec((None, tf, d), lambda i, j: (layer, first_tile_chunk(i, j), 0))],
        out_specs=[pl.BlockSpec((tm // 2, tf), lambda i, j: (i, j)),
                   pl.BlockSpec((None, tf, d), lambda i, j: (0, first_tile_chunk(i, j), 0))],
        out_shape=[jax.ShapeDtypeStruct((t // 2, ff), U32),
                   jax.ShapeDtypeStruct((1, ff, d), BF16)],
        scratch_shapes=[pltpu.VMEM((nf, SUBLANES, tf), F32),
                        pltpu.VMEM((nf, SUBLANES, tf), F32)],
        compiler_params=_params("arbitrary", "arbitrary"),
        name="ffn_up",
    )(xn, w_up, w_up, conv_w, conv_w, cb, cb, w_down)


def kernel(x, mem, norm_mix_g, w_in, w_out, mem_norm_g, w_mem_kv, mem_q_norm_g, mem_k_norm_g,
           s5_lam_re, s5_lam_im, s5_log_step, s5_b_re, s5_b_im, s5_c_re, s5_c_im, s5_d,
           s5_w_glu, kv_norm_g, w_kv_shared, norm_ffn_g, w_ffn_up, ffn_conv_w, ffn_conv_b,
           w_ffn_down):
    bsz, seq, d = x.shape
    depth = w_in.shape[0]
    n_a = s5_lam_re.shape[0]
    mem2 = mem.reshape(bsz * mem.shape[1], d)
    ssm_w = s5_w_glu.shape[1]
    mem_w = w_mem_kv.shape[2] // 2
    h = x.reshape(bsz * seq, d)
    kv_sh = None
    for i in range(depth):
        proj = matmul([rmsnorm(h, norm_mix_g[i])], w_in, i, tm=512, tn=1024, name="w_in")
        kv_mem = matmul([rmsnorm(mem2, mem_norm_g[i])], w_mem_kv, i, name="w_mem_kv")
        m_out = mem_attention(proj, kv_mem, mem_q_norm_g[i], mem_k_norm_g[i],
                              bsz=bsz, q_col_block=ssm_w // mem_w)
        if i < n_a:
            y = s5_scan(proj, s5_lam_re[i], s5_lam_im[i], s5_log_step[i], s5_b_re[i],
                        s5_b_im[i], s5_c_re[i], s5_c_im[i], s5_d[i], bsz=bsz)
            p_out = gelu_glu(y, s5_w_glu, i)
        else:
            p_out = stick_breaking(proj, kv_sh, bsz=bsz, heads=ssm_w // SB_HEAD_DIM)
        h = matmul([p_out, m_out], w_out, i, res=h, tm=512, tn=1024, name="w_out")
        act, w_down = ffn_up(rmsnorm(h, norm_ffn_g[i]), w_ffn_up, ffn_conv_w, ffn_conv_b,
                             w_ffn_down, i, seq=seq)
        h = matmul([act], w_down, 0, res=h, tm=512, tn=512, name="ffn_down")
        if i == n_a - 1:
            kv_sh = matmul([rmsnorm(h, kv_norm_g)], w_kv_shared[None], 0, out_dtype=BF16,
                           tm=512, tn=1024, name="w_kv_shared")
    return h.reshape(bsz, seq, d)
```
